```python
import math
import jax, jax.numpy as jnp
from jax import lax
import numpy as np


D_MODEL = 1024
BATCH = 2
SEQ = 16384
DEPTH = 1
DEC_BATCH = 4
DEC_SEQ = 8192
PAST_LEN = 128

QK_DIM = 64
V_DIM = 2 * QK_DIM
N_HEADS = D_MODEL // V_DIM
ATTN_WIDTH = N_HEADS * V_DIM
Q_WIDTH = N_HEADS * 2 * QK_DIM
ROT_DIM = QK_DIM // 4
ROPE_THETA = 500000.0
Q_BLOCK = 128
SSM_GROUP_SIZE = 16
SSM_WIDTH = D_MODEL // 2
SSM_GROUPS = SSM_WIDTH // SSM_GROUP_SIZE
SSM_STATE = 64
LOG_STEP_MIN = math.log(1e-3)
LOG_STEP_MAX = math.log(1e-1)
N_BRANCH = 2
IN_WIDTH = 2 * Q_WIDTH + ATTN_WIDTH + SSM_WIDTH + N_BRANCH * D_MODEL
SPLITS = (Q_WIDTH, 2 * Q_WIDTH, 2 * Q_WIDTH + ATTN_WIDTH, 2 * Q_WIDTH + ATTN_WIDTH + SSM_WIDTH)
D_FF = 4 * D_MODEL
N_MOD = 6
EPS = 1e-6

kernel_name = 'hybrid_diffattn_s5_encoder'


def rmsnorm(x, g):
    xf = x.astype(jnp.float32)
    y = xf * lax.rsqrt(jnp.mean(xf * xf, axis=-1, keepdims=True) + EPS) * g.astype(jnp.float32)
    return y.astype(x.dtype)


def rope_tables(L):
    pos = jnp.arange(L, dtype=jnp.float32)
    inv_freq = ROPE_THETA ** (-jnp.arange(0, ROT_DIM, 2, dtype=jnp.float32) / ROT_DIM)
    ang = pos[:, None] * inv_freq[None, :]
    return jnp.cos(ang)[None, :, None, None, :], jnp.sin(ang)[None, :, None, None, :]


def partial_rope(t, cos, sin):
    half = ROT_DIM // 2
    x1 = t[..., :half].astype(jnp.float32)
    x2 = t[..., half:ROT_DIM].astype(jnp.float32)
    rot = jnp.concatenate([x1 * cos - x2 * sin, x2 * cos + x1 * sin], axis=-1).astype(t.dtype)
    return jnp.concatenate([rot, t[..., ROT_DIM:]], axis=-1)


def diff_attention(q, k, v, lam, g_head, lam_init):
    Bn, L = q.shape[0], q.shape[1]
    nb = L // Q_BLOCK
    qb = (q * (QK_DIM ** -0.5)).reshape(Bn, nb, Q_BLOCK, N_HEADS, 2, QK_DIM).transpose(1, 0, 2, 3, 4, 5)

    def block(qblk):
        s = jnp.einsum('bqhmd,bkhmd->bhmqk', qblk, k, preferred_element_type=jnp.float32)
        p = jax.nn.softmax(s, axis=-1)
        a = p[:, :, 0] - lam * p[:, :, 1]
        return jnp.einsum('bhqk,bkhe->bqhe', a.astype(v.dtype), v)

    o = lax.map(block, qb)
    o = o.transpose(1, 0, 2, 3, 4).reshape(Bn, L, N_HEADS, V_DIM)
    o = rmsnorm(o, g_head) * (1.0 - lam_init)
    return o.reshape(Bn, L, ATTN_WIDTH)


def zoh(lam_re, lam_im, log_step, b_re, b_im):
    lam_re = lam_re.astype(jnp.float32)
    lam_im = lam_im.astype(jnp.float32)
    step = jnp.exp(log_step.astype(jnp.float32))[:, None]
    er = jnp.exp(lam_re * step)
    ar = er * jnp.cos(lam_im * step)
    ai = er * jnp.sin(lam_im * step)
    den = lam_re * lam_re + lam_im * lam_im
    nr = ar - 1.0
    cr = ((nr * lam_re + ai * lam_im) / den)[..., None]
    ci = ((ai * lam_re - nr * lam_im) / den)[..., None]
    b_re = b_re.astype(jnp.float32)
    b_im = b_im.astype(jnp.float32)
    return ar, ai, cr * b_re - ci * b_im, cr * b_im + ci * b_re


def complex_scan_op(e1, e2):
    a1r, a1i, b1r, b1i = e1
    a2r, a2i, b2r, b2i = e2
    return (a2r * a1r - a2i * a1i,
            a2r * a1i + a2i * a1r,
            a2r * b1r - a2i * b1i + b2r,
            a2r * b1i + a2i * b1r + b2i)


def s5_scan(u, lam_re, lam_im, log_step, b_re, b_im, c_re, c_im, reverse):
    ar, ai, bbr, bbi = zoh(lam_re, lam_im, log_step, b_re, b_im)
    bu_r = jnp.einsum('blgp,gnp->blgn', u, bbr)
    bu_i = jnp.einsum('blgp,gnp->blgn', u, bbi)
    a_r = jnp.broadcast_to(ar, bu_r.shape)
    a_i = jnp.broadcast_to(ai, bu_i.shape)
    _, _, sr, si = lax.associative_scan(complex_scan_op, (a_r, a_i, bu_r, bu_i), reverse=reverse, axis=1)
    return (jnp.einsum('blgn,gpn->blgp', sr, c_re.astype(jnp.float32))
            - jnp.einsum('blgn,gpn->blgp', si, c_im.astype(jnp.float32)))


def s5_branch(u, p):
    Bn, L, _ = u.shape
    uf = u.astype(jnp.float32).reshape(Bn, L, SSM_GROUPS, SSM_GROUP_SIZE)
    y_f = s5_scan(uf, p['ssm_lam_re_f'], p['ssm_lam_im_f'], p['ssm_log_step_f'], p['ssm_b_re'], p['ssm_b_im'],
                  p['ssm_c_re_f'], p['ssm_c_im_f'], False)
    y_b = s5_scan(uf, p['ssm_lam_re_b'], p['ssm_lam_im_b'], p['ssm_log_step_b'], p['ssm_b_re'], p['ssm_b_im'],
                  p['ssm_c_re_b'], p['ssm_c_im_b'], True)
    y = (y_f + y_b + uf * p['ssm_d'].astype(jnp.float32)).reshape(Bn, L, SSM_WIDTH)
    yg = jax.nn.gelu(y)
    out = yg * jax.nn.sigmoid(yg @ p['w_glu'].astype(jnp.float32) + p['b_glu'].astype(jnp.float32))
    return out.astype(u.dtype)


def mixer(h, p, lam_init):
    Bn, L, _ = h.shape
    z = h @ p['w_in']
    q, k, v, u, gates = jnp.split(z, SPLITS, axis=-1)
    cos, sin = rope_tables(L)
    q = partial_rope(q.reshape(Bn, L, N_HEADS, 2, QK_DIM), cos, sin)
    k = partial_rope(k.reshape(Bn, L, N_HEADS, 2, QK_DIM), cos, sin)
    v = v.reshape(Bn, L, N_HEADS, V_DIM)
    f32 = jnp.float32
    lam = (jnp.exp(jnp.sum(p['lam_q1'].astype(f32) * p['lam_k1'].astype(f32)))
           - jnp.exp(jnp.sum(p['lam_q2'].astype(f32) * p['lam_k2'].astype(f32))) + lam_init)
    attn = diff_attention(q, k, v, lam, p['g_head'], lam_init)
    ssm = s5_branch(u, p)
    g_a, g_s = jnp.split(jax.nn.sigmoid(gates.astype(f32)).astype(h.dtype), N_BRANCH, axis=-1)
    merged = g_a * (attn @ p['w_proj_attn']) + g_s * (ssm @ p['w_proj_ssm'])
    return merged @ p['w_out']


def encoder_layer(x, c, p, lam_init):
    mod = jax.nn.silu(c) @ p['w_ada'] + p['b_ada']
    sh1, sc1, gt1, sh2, sc2, gt2 = [m[:, None, :] for m in jnp.split(mod, N_MOD, axis=-1)]
    h = rmsnorm(x, p['g_pre_mix']) * (1 + sc1) + sh1
    x = x + gt1 * rmsnorm(mixer(h, p, lam_init), p['g_post_mix'])
    h = rmsnorm(x, p['g_pre_ffn']) * (1 + sc2) + sh2
    f = jnp.square(jax.nn.relu(h @ p['w_ff1'])) @ p['w_ff2']
    return x + gt2 * rmsnorm(f, p['g_post_ffn'])


def setup_inputs(seed: int = 0) -> dict:
    key = jax.random.key(seed)
    ks = iter(jax.random.split(key, 48))
    f32 = jnp.float32

    def nrm(shape, scale):
        return jax.random.normal(next(ks), shape, f32) * scale

    Ld, G, N, P = DEPTH, SSM_GROUPS, SSM_STATE, SSM_GROUP_SIZE
    n_idx = jnp.arange(N, dtype=f32)

    def lam_re():
        return -0.5 + nrm((Ld, G, N), 0.01)

    def lam_im():
        return math.pi * n_idx + nrm((Ld, G, N), 0.01)

    def log_step():
        return jax.random.uniform(next(ks), (Ld, G), f32, LOG_STEP_MIN, LOG_STEP_MAX)

    return {
        'x_prompt': nrm((BATCH, SEQ, D_MODEL), 1.0),
        'x_sample': nrm((DEC_BATCH, DEC_SEQ, D_MODEL), 1.0),
        'c_prompt': nrm((BATCH, D_MODEL), 1.0),
        'c_sample': nrm((DEC_BATCH, D_MODEL), 1.0),
        'w_ada': nrm((Ld, D_MODEL, N_MOD * D_MODEL), D_MODEL ** -0.5),
        'b_ada': nrm((Ld, N_MOD * D_MODEL), 0.02),
        'g_pre_mix': 1.0 + nrm((Ld, D_MODEL), 0.02),
        'g_post_mix': 1.0 + nrm((Ld, D_MODEL), 0.02),
        'g_pre_ffn': 1.0 + nrm((Ld, D_MODEL), 0.02),
        'g_post_ffn': 1.0 + nrm((Ld, D_MODEL), 0.02),
        'w_in': nrm((Ld, D_MODEL, IN_WIDTH), D_MODEL ** -0.5),
        'lam_q1': nrm((Ld, QK_DIM), 0.1),
        'lam_k1': nrm((Ld, QK_DIM), 0.1),
        'lam_q2': nrm((Ld, QK_DIM), 0.1),
        'lam_k2': nrm((Ld, QK_DIM), 0.1),
        'g_head': 1.0 + nrm((Ld, V_DIM), 0.02),
        'ssm_lam_re_f': lam_re(),
        'ssm_lam_im_f': lam_im(),
        'ssm_log_step_f': log_step(),
        'ssm_c_re_f': nrm((Ld, G, P, N), (2 * N) ** -0.5),
        'ssm_c_im_f': nrm((Ld, G, P, N), (2 * N) ** -0.5),
        'ssm_lam_re_b': lam_re(),
        'ssm_lam_im_b': lam_im(),
        'ssm_log_step_b': log_step(),
        'ssm_c_re_b': nrm((Ld, G, P, N), (2 * N) ** -0.5),
        'ssm_c_im_b': nrm((Ld, G, P, N), (2 * N) ** -0.5),
        'ssm_b_re': nrm((Ld, G, N, P), (2 * P) ** -0.5),
        'ssm_b_im': nrm((Ld, G, N, P), (2 * P) ** -0.5),
        'ssm_d': nrm((Ld, G, P), 1.0),
        'w_glu': nrm((Ld, SSM_WIDTH, SSM_WIDTH), SSM_WIDTH ** -0.5),
        'b_glu': nrm((Ld, SSM_WIDTH), 0.02),
        'w_proj_attn': nrm((Ld, ATTN_WIDTH, D_MODEL), ATTN_WIDTH ** -0.5),
        'w_proj_ssm': nrm((Ld, SSM_WIDTH, D_MODEL), SSM_WIDTH ** -0.5),
        'w_out': nrm((Ld, D_MODEL, D_MODEL), D_MODEL ** -0.5),
        'w_ff1': nrm((Ld, D_MODEL, D_FF), D_MODEL ** -0.5),
        'w_ff2': nrm((Ld, D_FF, D_MODEL), D_FF ** -0.5),
    }


def reference(x_prompt, x_sample, c_prompt, c_sample, w_ada, b_ada, g_pre_mix, g_post_mix, g_pre_ffn, g_post_ffn,
              w_in, lam_q1, lam_k1, lam_q2, lam_k2, g_head,
              ssm_lam_re_f, ssm_lam_im_f, ssm_log_step_f, ssm_c_re_f, ssm_c_im_f,
              ssm_lam_re_b, ssm_lam_im_b, ssm_log_step_b, ssm_c_re_b, ssm_c_im_b,
              ssm_b_re, ssm_b_im, ssm_d, w_glu, b_glu, w_proj_attn, w_proj_ssm, w_out, w_ff1, w_ff2):
    y_prompt = x_prompt
    y_sample = x_sample
    for l in range(DEPTH):
        p = dict(w_ada=w_ada[l], b_ada=b_ada[l], g_pre_mix=g_pre_mix[l], g_post_mix=g_post_mix[l],
                 g_pre_ffn=g_pre_ffn[l], g_post_ffn=g_post_ffn[l], w_in=w_in[l],
                 lam_q1=lam_q1[l], lam_k1=lam_k1[l], lam_q2=lam_q2[l], lam_k2=lam_k2[l], g_head=g_head[l],
                 ssm_lam_re_f=ssm_lam_re_f[l], ssm_lam_im_f=ssm_lam_im_f[l], ssm_log_step_f=ssm_log_step_f[l],
                 ssm_c_re_f=ssm_c_re_f[l], ssm_c_im_f=ssm_c_im_f[l],
                 ssm_lam_re_b=ssm_lam_re_b[l], ssm_lam_im_b=ssm_lam_im_b[l], ssm_log_step_b=ssm_log_step_b[l],
                 ssm_c_re_b=ssm_c_re_b[l], ssm_c_im_b=ssm_c_im_b[l],
                 ssm_b_re=ssm_b_re[l], ssm_b_im=ssm_b_im[l], ssm_d=ssm_d[l], w_glu=w_glu[l], b_glu=b_glu[l],
                 w_proj_attn=w_proj_attn[l], w_proj_ssm=w_proj_ssm[l], w_out=w_out[l],
                 w_ff1=w_ff1[l], w_ff2=w_ff2[l])
        lam_init = 0.8 - 0.6 * math.exp(-0.3 * l)
        y_prompt = encoder_layer(y_prompt, c_prompt, p, lam_init)
        y_sample = encoder_layer(y_sample, c_sample, p, lam_init)
    return (y_prompt, y_sample)
```

```python
import functools
import math

import jax
import jax.numpy as jnp
from jax import lax
from jax.experimental import pallas as pl
from jax.experimental.pallas import tpu as pltpu

F32 = jnp.float32
BF16 = jnp.bfloat16

QK_DIM = 64
V_DIM = 128
ROT_DIM = 16
ROT_HALF = ROT_DIM // 2
ROPE_THETA = 500000.0
SSM_GROUP_SIZE = 16
SSM_STATE = 64
N_MOD = 6
EPS = 1e-6
LOG2E = 1.4426950408889634

LANES = 128
MXU_DIM = 256
VMEM_LIMIT_BYTES = 56 * 1024 * 1024

TOKEN_TILE = 512
Q_TILE = 512
KV_TILE = 512
SSM_CHUNKS = 256
SSM_STEPS = 4
STATE_TILE = 256
SSM_OUT_TILE = 256


def _params(*semantics):
    return pltpu.CompilerParams(dimension_semantics=semantics, vmem_limit_bytes=VMEM_LIMIT_BYTES)


def _const_spec(shape):
    nd = len(shape)
    return pl.BlockSpec(shape, lambda *_: (0,) * nd, pipeline_mode=pl.Buffered(1))


def _rms(x):
    return x * lax.rsqrt(jnp.mean(x * x, axis=-1, keepdims=True) + EPS)


def _adaln_kernel(c_ref, w_ref, b_ref, o_ref):
    c = c_ref[...]
    a = c * jax.nn.sigmoid(c)
    w = w_ref[...]
    a_hi = a.astype(BF16)
    a_lo = (a - a_hi.astype(F32)).astype(BF16)
    w_hi = w.astype(BF16)
    w_lo = (w - w_hi.astype(F32)).astype(BF16)
    acc = jnp.dot(a_hi, w_hi, preferred_element_type=F32)
    acc += jnp.dot(a_lo, w_hi, preferred_element_type=F32)
    acc += jnp.dot(a_hi, w_lo, preferred_element_type=F32)
    o_ref[...] = acc + b_ref[...]


def _adaln_mod(c_all, w_ada, b_ada):
    rows, d = c_all.shape
    n = w_ada.shape[1]
    tn = n // 4
    return pl.pallas_call(
        _adaln_kernel,
        grid=(n // tn,),
        in_specs=[
            pl.BlockSpec((rows, d), lambda j: (0, 0)),
            pl.BlockSpec((d, tn), lambda j: (0, j)),
            pl.BlockSpec((1, tn), lambda j: (0, j)),
        ],
        out_specs=pl.BlockSpec((rows, tn), lambda j: (0, j)),
        out_shape=jax.ShapeDtypeStruct((rows, n), F32),
        compiler_params=_params("arbitrary"),
        name="adaln_mod",
    )(c_all, w_ada, b_ada.reshape(1, n))


def _in_proj_kernel(x_ref, mod_ref, g_ref, wq_ref, wkt_ref, wv_ref, wu_ref, wg_ref,
                    ct_ref, sa_ref, sb_ref, cost_ref, sint_ref,
                    q_ref, kt_ref, v_ref, u_ref, ga_ref, gs_ref, *, n_heads, q_scale):
    x = x_ref[0]
    h = _rms(x) * g_ref[...] * (1.0 + mod_ref[0, 1:2, :]) + mod_ref[0, 0:1, :]
    hb = h.astype(BF16)

    q = jnp.dot(hb, wq_ref[...], preferred_element_type=F32)
    ct, sa, sb = ct_ref[...], sa_ref[...], sb_ref[...]
    for hd in range(n_heads):
        t = q[:, hd * LANES:(hd + 1) * LANES]
        r = t * ct + pltpu.roll(t, LANES - ROT_HALF, 1) * sa + pltpu.roll(t, ROT_HALF, 1) * sb
        q_ref[0, hd] = (r * q_scale).astype(BF16)

    kt = lax.dot_general(wkt_ref[...], hb, (((1,), (1,)), ((), ())), preferred_element_type=F32)
    cos_t, sin_t = cost_ref[...], sint_ref[...]
    for hd in range(n_heads):
        for m in range(2):
            r0 = hd * LANES + m * QK_DIM
            x1 = kt[r0:r0 + ROT_HALF]
            x2 = kt[r0 + ROT_HALF:r0 + ROT_DIM]
            rot = jnp.concatenate([x1 * cos_t - x2 * sin_t, x2 * cos_t + x1 * sin_t], axis=0)
            kt_ref[0, hd, m * QK_DIM:m * QK_DIM + ROT_DIM, :] = rot.astype(BF16)
            kt_ref[0, hd, m * QK_DIM + ROT_DIM:(m + 1) * QK_DIM, :] = kt[r0 + ROT_DIM:r0 + QK_DIM].astype(BF16)

    v = jnp.dot(hb, wv_ref[...], preferred_element_type=F32)
    for hd in range(n_heads):
        v_ref[0, hd] = v[:, hd * LANES:(hd + 1) * LANES].astype(BF16)

    u_ref[0] = jnp.dot(hb, wu_ref[...], preferred_element_type=F32).astype(BF16)

    d = ga_ref.shape[-1]
    gates = jax.nn.sigmoid(jnp.dot(hb, wg_ref[...], preferred_element_type=F32))
    ga_ref[0] = gates[:, :d].astype(BF16)
    gs_ref[0] = gates[:, d:].astype(BF16)


def _in_proj(x, mod, g_pre, wq, wkt, wv, wu, wg, tables, *, n_heads):
    b, l, d = x.shape
    tm = min(TOKEN_TILE, l)
    ct, sa, sb, cos_t, sin_t = tables
    ssm_w = wu.shape[1]
    kern = functools.partial(_in_proj_kernel, n_heads=n_heads, q_scale=(QK_DIM ** -0.5) * LOG2E)
    tok = lambda w: pl.BlockSpec((1, tm, w), lambda bi, i: (bi, i, 0))
    return pl.pallas_call(
        kern,
        grid=(b, l // tm),
        in_specs=[
            tok(d),
            pl.BlockSpec((1, N_MOD, d), lambda bi, i: (bi, 0, 0)),
            _const_spec((1, d)),
            _const_spec(wq.shape), _const_spec(wkt.shape), _const_spec(wv.shape),
            _const_spec(wu.shape), _const_spec(wg.shape),
            pl.BlockSpec((tm, LANES), lambda bi, i: (i, 0)),
            pl.BlockSpec((tm, LANES), lambda bi, i: (i, 0)),
            pl.BlockSpec((tm, LANES), lambda bi, i: (i, 0)),
            pl.BlockSpec((ROT_HALF, tm), lambda bi, i: (0, i)),
            pl.BlockSpec((ROT_HALF, tm), lambda bi, i: (0, i)),
        ],
        out_specs=[
            pl.BlockSpec((1, n_heads, tm, LANES), lambda bi, i: (bi, 0, i, 0)),
            pl.BlockSpec((1, n_heads, LANES, tm), lambda bi, i: (bi, 0, 0, i)),
            pl.BlockSpec((1, n_heads, tm, LANES), lambda bi, i: (bi, 0, i, 0)),
            tok(ssm_w), tok(d), tok(d),
        ],
        out_shape=[
            jax.ShapeDtypeStruct((b, n_heads, l, LANES), BF16),
            jax.ShapeDtypeStruct((b, n_heads, LANES, l), BF16),
            jax.ShapeDtypeStruct((b, n_heads, l, LANES), BF16),
            jax.ShapeDtypeStruct((b, l, ssm_w), BF16),
            jax.ShapeDtypeStruct((b, l, d), BF16),
            jax.ShapeDtypeStruct((b, l, d), BF16),
        ],
        compiler_params=_params("parallel", "parallel"),
        name="in_proj",
    )(x, mod, g_pre, wq, wkt, wv, wu, wg, ct, sa, sb, cos_t, sin_t)


def _rope_tables(l):
    pos = jnp.arange(l, dtype=F32)
    inv_freq = ROPE_THETA ** (-jnp.arange(0, ROT_DIM, 2, dtype=F32) / ROT_DIM)
    ang = pos[:, None] * inv_freq[None, :]
    cos, sin = jnp.cos(ang), jnp.sin(ang)
    lane = jnp.arange(LANES) % QK_DIM
    cos_l = jnp.take(cos, lane % ROT_HALF, axis=1)
    sin_l = jnp.take(sin, lane % ROT_HALF, axis=1)
    ct = jnp.where(lane < ROT_DIM, cos_l, 1.0)
    sa = jnp.where(lane < ROT_HALF, -sin_l, 0.0)
    sb = jnp.where((lane >= ROT_HALF) & (lane < ROT_DIM), sin_l, 0.0)
    return ct, sa, sb, cos.T, sin.T


def _attn_kernel(lq1_ref, lk1_ref, lq2_ref, lk2_ref, gh_ref, q_ref, kt_ref, v_ref, o_ref,
                 m0_ref, m1_ref, acc0_ref, acc1_ref, *, tk, lam_init):
    q = q_ref[0, 0]
    q0 = q[:, :QK_DIM]
    q1 = q[:, QK_DIM:]
    nk = kt_ref.shape[-1] // tk
    tq = q.shape[0]
    ones = jnp.ones((tk, LANES), BF16)

    m0_ref[...] = jnp.full((tq, 1), -jnp.inf, F32)
    m1_ref[...] = jnp.full((tq, 1), -jnp.inf, F32)
    acc0_ref[...] = jnp.zeros((tq, 2 * LANES), F32)
    acc1_ref[...] = jnp.zeros((tq, 2 * LANES), F32)

    def one_map(qm, ktm, va, m_ref, acc_ref):
        s = jnp.dot(qm, ktm, preferred_element_type=F32)
        m_old = m_ref[...]
        m_new = jnp.maximum(m_old, jnp.max(s, axis=1, keepdims=True))
        p = jnp.exp2(s - m_new)
        alpha = jnp.exp2(m_old - m_new)
        acc_ref[...] = alpha * acc_ref[...] + jnp.dot(p.astype(BF16), va, preferred_element_type=F32)
        m_ref[...] = m_new

    def body(j, carry):
        off = pl.multiple_of(j * tk, tk)
        kt = kt_ref[0, 0, :, pl.ds(off, tk)]
        va = jnp.concatenate([v_ref[0, 0, pl.ds(off, tk), :], ones], axis=1)
        one_map(q0, kt[:QK_DIM], va, m0_ref, acc0_ref)
        one_map(q1, kt[QK_DIM:], va, m1_ref, acc1_ref)
        return carry

    lax.fori_loop(0, nk, body, 0)

    lam = (jnp.exp(jnp.sum(lq1_ref[...] * lk1_ref[...], axis=-1, keepdims=True))
           - jnp.exp(jnp.sum(lq2_ref[...] * lk2_ref[...], axis=-1, keepdims=True)) + lam_init)
    acc0 = acc0_ref[...]
    acc1 = acc1_ref[...]
    o = acc0[:, :LANES] / acc0[:, LANES:] - lam * (acc1[:, :LANES] / acc1[:, LANES:])
    o_ref[0] = (_rms(o) * gh_ref[...] * (1.0 - lam_init)).astype(BF16)


def _diff_attn(q, kt, v, lam_vecs, g_head, *, lam_init):
    b, n_heads, l, _ = q.shape
    tq = min(Q_TILE, l)
    tk = min(KV_TILE, l)
    kern = functools.partial(_attn_kernel, tk=tk, lam_init=lam_init)
    vec = _const_spec((1, QK_DIM))
    return pl.pallas_call(
        kern,
        grid=(b, n_heads, l // tq),
        in_specs=[
            vec, vec, vec, vec, _const_spec((1, V_DIM)),
            pl.BlockSpec((1, 1, tq, LANES), lambda bi, h, i: (bi, h, i, 0)),
            pl.BlockSpec((1, 1, LANES, l), lambda bi, h, i: (bi, h, 0, 0)),
            pl.BlockSpec((1, 1, l, LANES), lambda bi, h, i: (bi, h, 0, 0)),
        ],
        out_specs=pl.BlockSpec((1, tq, LANES), lambda bi, h, i: (bi, i, h)),
        out_shape=jax.ShapeDtypeStruct((b, l, n_heads * V_DIM), BF16),
        scratch_shapes=[
            pltpu.VMEM((tq, 1), F32), pltpu.VMEM((tq, 1), F32),
            pltpu.VMEM((tq, 2 * LANES), F32), pltpu.VMEM((tq, 2 * LANES), F32),
        ],
        compiler_params=_params("parallel", "parallel", "arbitrary"),
        name="diff_attn",
    )(*lam_vecs, g_head, q, kt, v)


def _zoh(lam_re, lam_im, log_step, b_re, b_im, n_steps):
    step = jnp.exp(log_step.astype(F32))[:, None]
    lam_re = lam_re.astype(F32)
    lam_im = lam_im.astype(F32)
    er = jnp.exp(lam_re * step)
    ar = er * jnp.cos(lam_im * step)
    ai = er * jnp.sin(lam_im * step)
    en = jnp.exp(lam_re * step * n_steps)
    anr = en * jnp.cos(lam_im * step * n_steps)
    ani = en * jnp.sin(lam_im * step * n_steps)
    den = lam_re * lam_re + lam_im * lam_im
    nr = ar - 1.0
    cr = ((nr * lam_re + ai * lam_im) / den)[..., None]
    ci = ((ai * lam_re - nr * lam_im) / den)[..., None]
    b_re = b_re.astype(F32)
    b_im = b_im.astype(F32)
    return ar, ai, anr, ani, cr * b_re - ci * b_im, cr * b_im + ci * b_re


def _ssm_in_tiles(bb):
    g, n, p = bb.shape
    eye = jnp.eye(g, dtype=F32)
    full = jnp.einsum("gnp,gh->gphn", bb, eye).reshape(g * p, g * n)
    tiles = []
    for k in range(g * n // STATE_TILE):
        r0 = (k * STATE_TILE // n) * p // LANES * LANES
        tiles.append(full[r0:r0 + LANES, k * STATE_TILE:(k + 1) * STATE_TILE])
    return jnp.stack(tiles).astype(BF16)


def _ssm_out_tiles(c):
    g, p, n = c.shape
    eye = jnp.eye(g, dtype=F32)
    full = jnp.einsum("gpn,gh->gnhp", c.astype(F32), eye).reshape(g * n, g * p)
    rows = SSM_OUT_TILE // p * n
    tiles = [full[k * rows:(k + 1) * rows, k * SSM_OUT_TILE:(k + 1) * SSM_OUT_TILE]
             for k in range(g * p // SSM_OUT_TILE)]
    return jnp.stack(tiles).astype(BF16)


def _s5_step(u_t, a_ref, bre_ref, bim_ref, xre_ref, xim_ref, d):
    n_tiles = bre_ref.shape[1]
    in_rows_per_tile = STATE_TILE // SSM_STATE * SSM_GROUP_SIZE
    for k in range(n_tiles):
        c0 = (k * in_rows_per_tile) // LANES * LANES
        uk = u_t[:, c0:c0 + LANES]
        sl = slice(k * STATE_TILE, (k + 1) * STATE_TILE)
        bu_re = jnp.dot(uk, bre_ref[d, k], preferred_element_type=F32)
        bu_im = jnp.dot(uk, bim_ref[d, k], preferred_element_type=F32)
        a_re = a_ref[d, 0:1, sl]
        a_im = a_ref[d, 1:2, sl]
        x_re = xre_ref[d, :, sl]
        x_im = xim_ref[d, :, sl]
        xre_ref[d, :, sl] = a_re * x_re - a_im * x_im + bu_re
        xim_ref[d, :, sl] = a_re * x_im + a_im * x_re + bu_im


def _s5_local_kernel(uf_ref, ub_ref, a_ref, bre_ref, bim_ref, s_ref, xre_ref, xim_ref, *, steps, width):
    j = pl.program_id(1)

    @pl.when(j == 0)
    def _():
        xre_ref[...] = jnp.zeros_like(xre_ref)
        xim_ref[...] = jnp.zeros_like(xim_ref)

    for t in range(steps):
        _s5_step(uf_ref[0, :, t * width:(t + 1) * width], a_ref, bre_ref, bim_ref, xre_ref, xim_ref, 0)
        tb = steps - 1 - t
        _s5_step(ub_ref[0, :, tb * width:(tb + 1) * width], a_ref, bre_ref, bim_ref, xre_ref, xim_ref, 1)

    @pl.when(j == pl.num_programs(1) - 1)
    def _():
        s_ref[0, 0] = xre_ref[0]
        s_ref[0, 1] = xim_ref[0]
        s_ref[0, 2] = xre_ref[1]
        s_ref[0, 3] = xim_ref[1]


def _s5_scan_kernel(uf_ref, ub_ref, x0_ref, a_ref, bre_ref, bim_ref, cre_ref, cim_ref,
                    yf_ref, yb_ref, xre_ref, xim_ref, *, steps, width):
    j = pl.program_id(1)

    @pl.when(j == 0)
    def _():
        xre_ref[0] = x0_ref[0, 0]
        xim_ref[0] = x0_ref[0, 1]
        xre_ref[1] = x0_ref[0, 2]
        xim_ref[1] = x0_ref[0, 3]

    def emit(y_ref, t, d):
        n_out = cre_ref.shape[1]
        rows = cre_ref.shape[2]
        for k in range(n_out):
            x_re = xre_ref[d, :, k * rows:(k + 1) * rows].astype(BF16)
            x_im = xim_ref[d, :, k * rows:(k + 1) * rows].astype(BF16)
            y = (jnp.dot(x_re, cre_ref[d, k], preferred_element_type=F32)
                 - jnp.dot(x_im, cim_ref[d, k], preferred_element_type=F32))
            c0 = t * width + k * SSM_OUT_TILE
            y_ref[0, :, c0:c0 + SSM_OUT_TILE] = y.astype(BF16)

    for t in range(steps):
        _s5_step(uf_ref[0, :, t * width:(t + 1) * width], a_ref, bre_ref, bim_ref, xre_ref, xim_ref, 0)
        emit(yf_ref, t, 0)
        tb = steps - 1 - t
        _s5_step(ub_ref[0, :, tb * width:(tb + 1) * width], a_ref, bre_ref, bim_ref, xre_ref, xim_ref, 1)
        emit(yb_ref, tb, 1)


def _s5_carry_kernel(s_ref, an_ref, x0_ref):
    d = pl.program_id(1)
    n_chunks = s_ref.shape[2]
    a_re = an_ref[0, 0:1, :]
    a_im = an_ref[0, 1:2, :]

    def body(i, carry):
        x_re, x_im = carry
        c = jnp.where(d == 0, i, n_chunks - 1 - i)
        x0_ref[0, 0, pl.ds(c, 1), :] = x_re
        x0_ref[0, 1, pl.ds(c, 1), :] = x_im
        s_re = s_ref[0, 0, pl.ds(c, 1), :]
        s_im = s_ref[0, 1, pl.ds(c, 1), :]
        return a_re * x_re - a_im * x_im + s_re, a_re * x_im + a_im * x_re + s_im

    zero = jnp.zeros((1, s_ref.shape[3]), F32)
    lax.fori_loop(0, n_chunks, body, (zero, zero))


def _s5_branch_core(u, ssm):
    b, l, w = u.shape
    a, a_n, bre, bim, cre, cim = ssm
    n_chunks = min(SSM_CHUNKS, l // SSM_STEPS)
    t_len = l // n_chunks
    steps = min(SSM_STEPS, t_len)
    n_t = t_len // steps
    n_state = a.shape[-1]
    u2 = u.reshape(b, n_chunks, t_len * w)
    blk = steps * w
    fwd = pl.BlockSpec((1, n_chunks, blk), lambda bi, j: (bi, 0, j))
    bwd = pl.BlockSpec((1, n_chunks, blk), lambda bi, j: (bi, 0, n_t - 1 - j))
    state = pl.BlockSpec((1, 4, n_chunks, n_state), lambda bi, j: (bi, 0, 0, 0))
    scratch = [pltpu.VMEM((2, n_chunks, n_state), F32), pltpu.VMEM((2, n_chunks, n_state), F32)]

    s_end = pl.pallas_call(
        functools.partial(_s5_local_kernel, steps=steps, width=w),
        grid=(b, n_t),
        in_specs=[fwd, bwd, _const_spec(a.shape), _const_spec(bre.shape), _const_spec(bim.shape)],
        out_specs=state,
        out_shape=jax.ShapeDtypeStruct((b, 4, n_chunks, n_state), F32),
        scratch_shapes=scratch,
        compiler_params=_params("parallel", "arbitrary"),
        name="s5_local",
    )(u2, u2, a, bre, bim)

    x0 = pl.pallas_call(
        _s5_carry_kernel,
        grid=(b, 2),
        in_specs=[
            pl.BlockSpec((1, 2, n_chunks, n_state), lambda bi, d: (bi, d, 0, 0)),
            pl.BlockSpec((1, 2, n_state), lambda bi, d: (d, 0, 0)),
        ],
        out_specs=pl.BlockSpec((1, 2, n_chunks, n_state), lambda bi, d: (bi, d, 0, 0)),
        out_shape=jax.ShapeDtypeStruct((b, 4, n_chunks, n_state), F32),
        compiler_params=_params("parallel", "arbitrary"),
        name="s5_carry",
    )(s_end, a_n)

    yf, yb = pl.pallas_call(
        functools.partial(_s5_scan_kernel, steps=steps, width=w),
        grid=(b, n_t),
        in_specs=[fwd, bwd, state, _const_spec(a.shape), _const_spec(bre.shape), _const_spec(bim.shape),
                  _const_spec(cre.shape), _const_spec(cim.shape)],
        out_specs=[fwd, bwd],
        out_shape=[jax.ShapeDtypeStruct(u2.shape, BF16), jax.ShapeDtypeStruct(u2.shape, BF16)],
        scratch_shapes=scratch,
        compiler_params=_params("parallel", "arbitrary"),
        name="s5_scan",
    )(u2, u2, x0, a, bre, bim, cre, cim)
    return yf.reshape(b, l, w), yb.reshape(b, l, w)


def _ssm_params(p, n_steps):
    zf = _zoh(p["ssm_lam_re_f"], p["ssm_lam_im_f"], p["ssm_log_step_f"], p["ssm_b_re"], p["ssm_b_im"], n_steps)
    zb = _zoh(p["ssm_lam_re_b"], p["ssm_lam_im_b"], p["ssm_log_step_b"], p["ssm_b_re"], p["ssm_b_im"], n_steps)
    flat = lambda z: z.reshape(1, -1)
    a = jnp.stack([jnp.concatenate([flat(z[0]), flat(z[1])]) for z in (zf, zb)])
    a_n = jnp.stack([jnp.concatenate([flat(z[2]), flat(z[3])]) for z in (zf, zb)])
    bre = jnp.stack([_ssm_in_tiles(zf[4]), _ssm_in_tiles(zb[4])])
    bim = jnp.stack([_ssm_in_tiles(zf[5]), _ssm_in_tiles(zb[5])])
    cre = jnp.stack([_ssm_out_tiles(p["ssm_c_re_f"]), _ssm_out_tiles(p["ssm_c_re_b"])])
    cim = jnp.stack([_ssm_out_tiles(p["ssm_c_im_f"]), _ssm_out_tiles(p["ssm_c_im_b"])])
    return a, a_n, bre, bim, cre, cim


def _merge_kernel(x_ref, mod_ref, u_ref, yf_ref, yb_ref, attn_ref, ga_ref, gs_ref,
                  dskip_ref, wglu_ref, bglu_ref, wpa_ref, wps_ref, wout_ref, gpost_ref, o_ref):
    u = u_ref[0].astype(F32)
    y = yf_ref[0].astype(F32) + yb_ref[0].astype(F32) + u * dskip_ref[...]
    yg = jax.nn.gelu(y)
    gate = jax.nn.sigmoid(jnp.dot(yg.astype(BF16), wglu_ref[...], preferred_element_type=F32) + bglu_ref[...])
    ssm = (yg * gate).astype(BF16)
    pa = jnp.dot(attn_ref[0], wpa_ref[...], preferred_element_type=F32)
    ps = jnp.dot(ssm, wps_ref[...], preferred_element_type=F32)
    merged = ga_ref[0].astype(F32) * pa + gs_ref[0].astype(F32) * ps
    mixed = jnp.dot(merged.astype(BF16), wout_ref[...], preferred_element_type=F32)
    o_ref[0] = x_ref[0] + mod_ref[0, 2:3, :] * (_rms(mixed) * gpost_ref[...])


def _merge(x, mod, u, yf, yb, attn, ga, gs, dskip, wglu, bglu, wpa, wps, wout, gpost):
    b, l, d = x.shape
    w = u.shape[-1]
    tm = min(TOKEN_TILE, l)
    tok = lambda width: pl.BlockSpec((1, tm, width), lambda bi, i: (bi, i, 0))
    return pl.pallas_call(
        _merge_kernel,
        grid=(b, l // tm),
        in_specs=[
            tok(d), pl.BlockSpec((1, N_MOD, d), lambda bi, i: (bi, 0, 0)),
            tok(w), tok(w), tok(w), tok(d), tok(d), tok(d),
            _const_spec((1, w)), _const_spec(wglu.shape), _const_spec((1, w)),
            _const_spec(wpa.shape), _const_spec(wps.shape), _const_spec(wout.shape), _const_spec((1, d)),
        ],
        out_specs=tok(d),
        out_shape=jax.ShapeDtypeStruct((b, l, d), F32),
        compiler_params=_params("parallel", "parallel"),
        name="merge",
    )(x, mod, u, yf, yb, attn, ga, gs, dskip, wglu, bglu, wpa, wps, wout, gpost)


def _ffn_kernel(x_ref, mod_ref, gpre_ref, w1_ref, w2_ref, gpost_ref, o_ref):
    x = x_ref[0]
    h = _rms(x) * gpre_ref[...] * (1.0 + mod_ref[0, 4:5, :]) + mod_ref[0, 3:4, :]
    a = jnp.dot(h.astype(BF16), w1_ref[...], preferred_element_type=F32)
    a = jnp.square(jnp.maximum(a, 0.0)).astype(BF16)
    f = jnp.dot(a, w2_ref[...], preferred_element_type=F32)
    o_ref[0] = x + mod_ref[0, 5:6, :] * (_rms(f) * gpost_ref[...])


def _ffn(x, mod, gpre, w1, w2, gpost):
    b, l, d = x.shape
    tm = min(TOKEN_TILE, l)
    tok = pl.BlockSpec((1, tm, d), lambda bi, i: (bi, i, 0))
    return pl.pallas_call(
        _ffn_kernel,
        grid=(b, l // tm),
        in_specs=[tok, pl.BlockSpec((1, N_MOD, d), lambda bi, i: (bi, 0, 0)), _const_spec((1, d)),
                  _const_spec(w1.shape), _const_spec(w2.shape), _const_spec((1, d))],
        out_specs=tok,
        out_shape=jax.ShapeDtypeStruct((b, l, d), F32),
        compiler_params=_params("parallel", "parallel"),
        name="ffn",
    )(x, mod, gpre, w1, w2, gpost)


def _encoder_layer(x, mod, p, lam_init):
    b, l, d = x.shape
    n_heads = d // V_DIM
    tables = _rope_tables(l)
    q, kt, v, u, ga, gs = _in_proj(x, mod, p["g_pre_mix"], p["wq"], p["wkt"], p["wv"], p["wu"], p["wg"],
                                   tables, n_heads=n_heads)
    attn = _diff_attn(q, kt, v, p["lam_vecs"], p["g_head"], lam_init=lam_init)
    n_chunks = min(SSM_CHUNKS, l // SSM_STEPS)
    yf, yb = _s5_branch_core(u, _ssm_params(p, l // n_chunks))
    x1 = _merge(x, mod, u, yf, yb, attn, ga, gs, p["ssm_d"], p["w_glu"], p["b_glu"],
                p["w_proj_attn"], p["w_proj_ssm"], p["w_out"], p["g_post_mix"])
    return _ffn(x1, mod, p["g_pre_ffn"], p["w_ff1"], p["w_ff2"], p["g_post_ffn"])


def kernel(x_prompt, x_sample, c_prompt, c_sample, w_ada, b_ada, g_pre_mix, g_post_mix, g_pre_ffn, g_post_ffn, w_in, lam_q1, lam_k1, lam_q2, lam_k2, g_head, ssm_lam_re_f, ssm_lam_im_f, ssm_log_step_f, ssm_c_re_f, ssm_c_im_f, ssm_lam_re_b, ssm_lam_im_b, ssm_log_step_b, ssm_c_re_b, ssm_c_im_b, ssm_b_re, ssm_b_im, ssm_d, w_glu, b_glu, w_proj_attn, w_proj_ssm, w_out, w_ff1, w_ff2):
    depth = w_in.shape[0]
    d = x_prompt.shape[-1]
    q_width = 2 * QK_DIM * (d // V_DIM)
    ssm_width = w_glu.shape[-1]
    splits = (q_width, 2 * q_width, 2 * q_width + d, 2 * q_width + d + ssm_width)
    n_p, n_s = c_prompt.shape[0], c_sample.shape[0]
    pad = -(n_p + n_s) % 16
    c_all = jnp.concatenate([c_prompt, c_sample, jnp.zeros((pad, d), F32)], axis=0)

    y_prompt, y_sample = x_prompt, x_sample
    for layer in range(depth):
        row = lambda t: t[layer].reshape(1, -1).astype(F32)
        w_in_l = w_in[layer]
        p = dict(
            g_pre_mix=row(g_pre_mix), g_post_mix=row(g_post_mix), g_pre_ffn=row(g_pre_ffn),
            g_post_ffn=row(g_post_ffn), g_head=row(g_head),
            wq=w_in_l[:, :splits[0]].astype(BF16),
            wkt=w_in_l[:, splits[0]:splits[1]].T.astype(BF16),
            wv=w_in_l[:, splits[1]:splits[2]].astype(BF16),
            wu=w_in_l[:, splits[2]:splits[3]].astype(BF16),
            wg=w_in_l[:, splits[3]:].astype(BF16),
            lam_vecs=(row(lam_q1), row(lam_k1), row(lam_q2), row(lam_k2)),
            ssm_lam_re_f=ssm_lam_re_f[layer], ssm_lam_im_f=ssm_lam_im_f[layer], ssm_log_step_f=ssm_log_step_f[layer],
            ssm_c_re_f=ssm_c_re_f[layer], ssm_c_im_f=ssm_c_im_f[layer],
            ssm_lam_re_b=ssm_lam_re_b[layer], ssm_lam_im_b=ssm_lam_im_b[layer], ssm_log_step_b=ssm_log_step_b[layer],
            ssm_c_re_b=ssm_c_re_b[layer], ssm_c_im_b=ssm_c_im_b[layer],
            ssm_b_re=ssm_b_re[layer], ssm_b_im=ssm_b_im[layer],
            ssm_d=row(ssm_d), w_glu=w_glu[layer].astype(BF16), b_glu=row(b_glu),
            w_proj_attn=w_proj_attn[layer].astype(BF16), w_proj_ssm=w_proj_ssm[layer].astype(BF16),
            w_out=w_out[layer].astype(BF16), w_ff1=w_ff1[layer].astype(BF16), w_ff2=w_ff2[layer].astype(BF16),
        )
        lam_init = 0.8 - 0.6 * math.exp(-0.3 * layer)
        mod = _adaln_mod(c_all, w_ada[layer], b_ada[layer])[:n_p + n_s].reshape(n_p + n_s, N_MOD, d)
        y_prompt = _encoder_layer(y_prompt, mod[:n_p], p, lam_init)
        y_sample = _encoder_layer(y_sample, mod[n_p:], p, lam_init)
    return (y_prompt, y_sample)
```

```python
import functools
import math

import jax
import jax.numpy as jnp
from jax import lax
from jax.experimental import pallas as pl
from jax.experimental.pallas import tpu as pltpu

F32 = jnp.float32
BF16 = jnp.bfloat16

QK_DIM = 64
V_DIM = 128
ROT_DIM = 16
ROT_HALF = ROT_DIM // 2
ROPE_THETA = 500000.0
SSM_GROUP_SIZE = 16
SSM_STATE = 64
N_MOD = 6
EPS = 1e-6
LOG2E = 1.4426950408889634

LANES = 128
MXU_DIM = 256
VMEM_LIMIT_BYTES = 56 * 1024 * 1024

TOKEN_TILE = 512
Q_TILE = 512
KV_TILE = 2048
SSM_CHUNKS = 256
SSM_STEPS = 4
STATE_TILE = 256
SSM_OUT_TILE = 256

SHIFT_HEADROOM = 30.0
SHIFT_MAX_GAP = 90.0
FIRST_KEYS = 128
NORM_CHUNK = 2048


def _params(*semantics):
    return pltpu.CompilerParams(dimension_semantics=semantics, vmem_limit_bytes=VMEM_LIMIT_BYTES)


def _const_spec(shape):
    nd = len(shape)
    return pl.BlockSpec(shape, lambda *_: (0,) * nd, pipeline_mode=pl.Buffered(1))


def _rms(x):
    return x * lax.rsqrt(jnp.mean(x * x, axis=-1, keepdims=True) + EPS)


def _adaln_kernel(c_ref, w_ref, b_ref, o_ref):
    c = c_ref[...]
    a = c * jax.nn.sigmoid(c)
    w = w_ref[...]
    a_hi = a.astype(BF16)
    a_lo = (a - a_hi.astype(F32)).astype(BF16)
    w_hi = w.astype(BF16)
    w_lo = (w - w_hi.astype(F32)).astype(BF16)
    acc = jnp.dot(a_hi, w_hi, preferred_element_type=F32)
    acc += jnp.dot(a_lo, w_hi, preferred_element_type=F32)
    acc += jnp.dot(a_hi, w_lo, preferred_element_type=F32)
    o_ref[...] = acc + b_ref[...]


def _adaln_mod(c_all, w_ada, b_ada):
    rows, d = c_all.shape
    n = w_ada.shape[1]
    tn = n // 4
    return pl.pallas_call(
        _adaln_kernel,
        grid=(n // tn,),
        in_specs=[
            pl.BlockSpec((rows, d), lambda j: (0, 0)),
            pl.BlockSpec((d, tn), lambda j: (0, j)),
            pl.BlockSpec((1, tn), lambda j: (0, j)),
        ],
        out_specs=pl.BlockSpec((rows, tn), lambda j: (0, j)),
        out_shape=jax.ShapeDtypeStruct((rows, n), F32),
        compiler_params=_params("arbitrary"),
        name="adaln_mod",
    )(c_all, w_ada, b_ada.reshape(1, n))


def _in_proj_kernel(x_ref, mod_ref, g_ref, wq_ref, wkt_ref, wv_ref, wu_ref, wg_ref,
                    ct_ref, sa_ref, sb_ref, cost_ref, sint_ref,
                    q_ref, kt_ref, v_ref, u_ref, ga_ref, gs_ref, *, n_heads, q_scale):
    x = x_ref[0]
    h = _rms(x) * g_ref[...] * (1.0 + mod_ref[0, 1:2, :]) + mod_ref[0, 0:1, :]
    hb = h.astype(BF16)

    q = jnp.dot(hb, wq_ref[...], preferred_element_type=F32)
    ct, sa, sb = ct_ref[...], sa_ref[...], sb_ref[...]
    for hd in range(n_heads):
        t = q[:, hd * LANES:(hd + 1) * LANES]
        r = t * ct + pltpu.roll(t, LANES - ROT_HALF, 1) * sa + pltpu.roll(t, ROT_HALF, 1) * sb
        q_ref[0, hd] = (r * q_scale).astype(BF16)

    kt = lax.dot_general(wkt_ref[...], hb, (((1,), (1,)), ((), ())), preferred_element_type=F32)
    cos_t, sin_t = cost_ref[...], sint_ref[...]
    for hd in range(n_heads):
        for m in range(2):
            r0 = hd * LANES + m * QK_DIM
            x1 = kt[r0:r0 + ROT_HALF]
            x2 = kt[r0 + ROT_HALF:r0 + ROT_DIM]
            rot = jnp.concatenate([x1 * cos_t - x2 * sin_t, x2 * cos_t + x1 * sin_t], axis=0)
            kt_ref[0, hd, m * QK_DIM:m * QK_DIM + ROT_DIM, :] = rot.astype(BF16)
            kt_ref[0, hd, m * QK_DIM + ROT_DIM:(m + 1) * QK_DIM, :] = kt[r0 + ROT_DIM:r0 + QK_DIM].astype(BF16)

    v = jnp.dot(hb, wv_ref[...], preferred_element_type=F32)
    for hd in range(n_heads):
        v_ref[0, hd] = v[:, hd * LANES:(hd + 1) * LANES].astype(BF16)

    u_ref[0] = jnp.dot(hb, wu_ref[...], preferred_element_type=F32).astype(BF16)

    d = ga_ref.shape[-1]
    gates = jax.nn.sigmoid(jnp.dot(hb, wg_ref[...], preferred_element_type=F32))
    ga_ref[0] = gates[:, :d].astype(BF16)
    gs_ref[0] = gates[:, d:].astype(BF16)


def _in_proj(x, mod, g_pre, wq, wkt, wv, wu, wg, tables, *, n_heads):
    b, l, d = x.shape
    tm = min(TOKEN_TILE, l)
    ct, sa, sb, cos_t, sin_t = tables
    ssm_w = wu.shape[1]
    kern = functools.partial(_in_proj_kernel, n_heads=n_heads, q_scale=(QK_DIM ** -0.5) * LOG2E)
    tok = lambda w: pl.BlockSpec((1, tm, w), lambda bi, i: (bi, i, 0))
    return pl.pallas_call(
        kern,
        grid=(b, l // tm),
        in_specs=[
            tok(d),
            pl.BlockSpec((1, N_MOD, d), lambda bi, i: (bi, 0, 0)),
            _const_spec((1, d)),
            _const_spec(wq.shape), _const_spec(wkt.shape), _const_spec(wv.shape),
            _const_spec(wu.shape), _const_spec(wg.shape),
            pl.BlockSpec((tm, LANES), lambda bi, i: (i, 0)),
            pl.BlockSpec((tm, LANES), lambda bi, i: (i, 0)),
            pl.BlockSpec((tm, LANES), lambda bi, i: (i, 0)),
            pl.BlockSpec((ROT_HALF, tm), lambda bi, i: (0, i)),
            pl.BlockSpec((ROT_HALF, tm), lambda bi, i: (0, i)),
        ],
        out_specs=[
            pl.BlockSpec((1, n_heads, tm, LANES), lambda bi, i: (bi, 0, i, 0)),
            pl.BlockSpec((1, n_heads, LANES, tm), lambda bi, i: (bi, 0, 0, i)),
            pl.BlockSpec((1, n_heads, tm, LANES), lambda bi, i: (bi, 0, i, 0)),
            tok(ssm_w), tok(d), tok(d),
        ],
        out_shape=[
            jax.ShapeDtypeStruct((b, n_heads, l, LANES), BF16),
            jax.ShapeDtypeStruct((b, n_heads, LANES, l), BF16),
            jax.ShapeDtypeStruct((b, n_heads, l, LANES), BF16),
            jax.ShapeDtypeStruct((b, l, ssm_w), BF16),
            jax.ShapeDtypeStruct((b, l, d), BF16),
            jax.ShapeDtypeStruct((b, l, d), BF16),
        ],
        compiler_params=_params("parallel", "parallel"),
        name="in_proj",
    )(x, mod, g_pre, wq, wkt, wv, wu, wg, ct, sa, sb, cos_t, sin_t)


def _rope_tables(l):
    pos = jnp.arange(l, dtype=F32)
    inv_freq = ROPE_THETA ** (-jnp.arange(0, ROT_DIM, 2, dtype=F32) / ROT_DIM)
    ang = pos[:, None] * inv_freq[None, :]
    cos, sin = jnp.cos(ang), jnp.sin(ang)
    lane = jnp.arange(LANES) % QK_DIM
    cos_l = jnp.take(cos, lane % ROT_HALF, axis=1)
    sin_l = jnp.take(sin, lane % ROT_HALF, axis=1)
    ct = jnp.where(lane < ROT_DIM, cos_l, 1.0)
    sa = jnp.where(lane < ROT_HALF, -sin_l, 0.0)
    sb = jnp.where((lane >= ROT_HALF) & (lane < ROT_DIM), sin_l, 0.0)
    return ct, sa, sb, cos.T, sin.T


def _attn_kernel(lq1_ref, lk1_ref, lq2_ref, lk2_ref, gh_ref, q_ref, kt_ref, v_ref, o_ref,
                 kn_ref, m_ref, acc_ref, *, tk, lam_init):
    tq = q_ref.shape[2]
    l_keys = kt_ref.shape[-1]
    nk = l_keys // tk

    @pl.when(pl.program_id(2) == 0)
    def _key_norms():
        chunk = min(NORM_CHUNK, l_keys)

        def nbody(c, carry):
            off = pl.multiple_of(c * chunk, chunk)
            kt = kt_ref[0, 0, :, pl.ds(off, chunk)].astype(F32)
            sq = kt * kt
            return (jnp.maximum(carry[0], jnp.sum(sq[:QK_DIM], axis=0, keepdims=True)),
                    jnp.maximum(carry[1], jnp.sum(sq[QK_DIM:], axis=0, keepdims=True)))

        zero = jnp.zeros((1, chunk), F32)
        n0, n1 = lax.fori_loop(0, l_keys // chunk, nbody, (zero, zero))
        kn_ref[0:1, :] = jnp.broadcast_to(jnp.sqrt(jnp.max(n0, axis=1, keepdims=True)), (1, LANES))
        kn_ref[1:2, :] = jnp.broadcast_to(jnp.sqrt(jnp.max(n1, axis=1, keepdims=True)), (1, LANES))

    q = q_ref[0, 0]
    first_map = lax.broadcasted_iota(jnp.int32, (tq, LANES), 1) < QK_DIM
    zero = jnp.zeros_like(q)
    q_maps = (jnp.where(first_map, q, zero), jnp.where(first_map, zero, q))
    qf = q.astype(F32)
    sq = qf * qf
    kt_first = kt_ref[0, 0, :, 0:FIRST_KEYS]
    gap = None
    for mi in range(2):
        in_map = first_map if mi == 0 else jnp.logical_not(first_map)
        q_norm = jnp.sqrt(jnp.sum(jnp.where(in_map, sq, 0.0), axis=1, keepdims=True))
        upper = q_norm * kn_ref[mi:mi + 1, 0:1]
        attained = jnp.max(jnp.dot(q_maps[mi], kt_first, preferred_element_type=F32), axis=1, keepdims=True)
        m_ref[mi] = jnp.broadcast_to(jnp.maximum(attained, upper - SHIFT_HEADROOM), (tq, LANES))
        g = upper - attained
        gap = g if gap is None else jnp.maximum(gap, g)

    @pl.when(jnp.max(gap) > SHIFT_MAX_GAP)
    def _exact_shift():
        def mbody(j, carry):
            off = pl.multiple_of(j * tk, tk)
            kt = kt_ref[0, 0, :, pl.ds(off, tk)]
            out = []
            for mi in range(2):
                s = jnp.dot(q_maps[mi], kt, preferred_element_type=F32)
                mx = carry[mi]
                for c in range(tk // LANES):
                    mx = jnp.maximum(mx, s[:, c * LANES:(c + 1) * LANES])
                out.append(mx)
            return tuple(out)

        neg = jnp.full((tq, LANES), -jnp.inf, F32)
        mx = lax.fori_loop(0, nk, mbody, (neg, neg))
        for mi in range(2):
            m_ref[mi] = jnp.broadcast_to(jnp.max(mx[mi], axis=1, keepdims=True), (tq, LANES))

    acc_ref[...] = jnp.zeros_like(acc_ref)
    ones = jnp.ones((tk, LANES), BF16)

    def body(j, carry):
        off = pl.multiple_of(j * tk, tk)
        kt = kt_ref[0, 0, :, pl.ds(off, tk)]
        va = jnp.concatenate([v_ref[0, 0, pl.ds(off, tk), :], ones], axis=1)
        for mi in range(2):
            s = jnp.dot(q_maps[mi], kt, preferred_element_type=F32)
            m = m_ref[mi]
            p = jnp.concatenate([jnp.exp2(s[:, c * LANES:(c + 1) * LANES] - m) for c in range(tk // LANES)],
                                axis=1)
            acc_ref[mi] += jnp.dot(p.astype(BF16), va, preferred_element_type=F32)
        return carry

    lax.fori_loop(0, nk, body, 0)

    lam = (jnp.exp(jnp.sum(lq1_ref[...] * lk1_ref[...], axis=-1, keepdims=True))
           - jnp.exp(jnp.sum(lq2_ref[...] * lk2_ref[...], axis=-1, keepdims=True)) + lam_init)
    acc0 = acc_ref[0]
    acc1 = acc_ref[1]
    o = acc0[:, :LANES] / acc0[:, LANES:] - lam * (acc1[:, :LANES] / acc1[:, LANES:])
    o_ref[0] = (_rms(o) * gh_ref[...] * (1.0 - lam_init)).astype(BF16)


def _diff_attn(q, kt, v, lam_vecs, g_head, *, lam_init):
    b, n_heads, l, _ = q.shape
    tq = min(Q_TILE, l)
    tk = min(KV_TILE, l)
    kern = functools.partial(_attn_kernel, tk=tk, lam_init=lam_init)
    vec = _const_spec((1, QK_DIM))
    return pl.pallas_call(
        kern,
        grid=(b, n_heads, l // tq),
        in_specs=[
            vec, vec, vec, vec, _const_spec((1, V_DIM)),
            pl.BlockSpec((1, 1, tq, LANES), lambda bi, h, i: (bi, h, i, 0)),
            pl.BlockSpec((1, 1, LANES, l), lambda bi, h, i: (bi, h, 0, 0)),
            pl.BlockSpec((1, 1, l, LANES), lambda bi, h, i: (bi, h, 0, 0)),
        ],
        out_specs=pl.BlockSpec((1, tq, LANES), lambda bi, h, i: (bi, i, h)),
        out_shape=jax.ShapeDtypeStruct((b, l, n_heads * V_DIM), BF16),
        scratch_shapes=[
            pltpu.VMEM((2, LANES), F32),
            pltpu.VMEM((2, tq, LANES), F32),
            pltpu.VMEM((2, tq, 2 * LANES), F32),
        ],
        compiler_params=_params("parallel", "parallel", "arbitrary"),
        name="diff_attn",
    )(*lam_vecs, g_head, q, kt, v)


def _zoh(lam_re, lam_im, log_step, b_re, b_im, n_steps):
    step = jnp.exp(log_step.astype(F32))[:, None]
    lam_re = lam_re.astype(F32)
    lam_im = lam_im.astype(F32)
    er = jnp.exp(lam_re * step)
    ar = er * jnp.cos(lam_im * step)
    ai = er * jnp.sin(lam_im * step)
    en = jnp.exp(lam_re * step * n_steps)
    anr = en * jnp.cos(lam_im * step * n_steps)
    ani = en * jnp.sin(lam_im * step * n_steps)
    den = lam_re * lam_re + lam_im * lam_im
    nr = ar - 1.0
    cr = ((nr * lam_re + ai * lam_im) / den)[..., None]
    ci = ((ai * lam_re - nr * lam_im) / den)[..., None]
    b_re = b_re.astype(F32)
    b_im = b_im.astype(F32)
    return ar, ai, anr, ani, cr * b_re - ci * b_im, cr * b_im + ci * b_re


def _ssm_in_tiles(bb):
    g, n, p = bb.shape
    eye = jnp.eye(g, dtype=F32)
    full = jnp.einsum("gnp,gh->gphn", bb, eye).reshape(g * p, g * n)
    tiles = []
    for k in range(g * n // STATE_TILE):
        r0 = (k * STATE_TILE // n) * p // LANES * LANES
        tiles.append(full[r0:r0 + LANES, k * STATE_TILE:(k + 1) * STATE_TILE])
    return jnp.stack(tiles).astype(BF16)


def _ssm_out_tiles(c):
    g, p, n = c.shape
    eye = jnp.eye(g, dtype=F32)
    full = jnp.einsum("gpn,gh->gnhp", c.astype(F32), eye).reshape(g * n, g * p)
    rows = SSM_OUT_TILE // p * n
    tiles = [full[k * rows:(k + 1) * rows, k * SSM_OUT_TILE:(k + 1) * SSM_OUT_TILE]
             for k in range(g * p // SSM_OUT_TILE)]
    return jnp.stack(tiles).astype(BF16)


def _s5_step(u_t, a_ref, bre_ref, bim_ref, xre_ref, xim_ref, d):
    n_tiles = bre_ref.shape[1]
    in_rows_per_tile = STATE_TILE // SSM_STATE * SSM_GROUP_SIZE
    for k in range(n_tiles):
        c0 = (k * in_rows_per_tile) // LANES * LANES
        uk = u_t[:, c0:c0 + LANES]
        sl = slice(k * STATE_TILE, (k + 1) * STATE_TILE)
        bu_re = jnp.dot(uk, bre_ref[d, k], preferred_element_type=F32)
        bu_im = jnp.dot(uk, bim_ref[d, k], preferred_element_type=F32)
        a_re = a_ref[d, 0:1, sl]
        a_im = a_ref[d, 1:2, sl]
        x_re = xre_ref[d, :, sl]
        x_im = xim_ref[d, :, sl]
        xre_ref[d, :, sl] = a_re * x_re - a_im * x_im + bu_re
        xim_ref[d, :, sl] = a_re * x_im + a_im * x_re + bu_im


def _s5_local_kernel(uf_ref, ub_ref, a_ref, bre_ref, bim_ref, s_ref, xre_ref, xim_ref, *, steps, width):
    j = pl.program_id(1)

    @pl.when(j == 0)
    def _():
        xre_ref[...] = jnp.zeros_like(xre_ref)
        xim_ref[...] = jnp.zeros_like(xim_ref)

    for t in range(steps):
        _s5_step(uf_ref[0, :, t * width:(t + 1) * width], a_ref, bre_ref, bim_ref, xre_ref, xim_ref, 0)
        tb = steps - 1 - t
        _s5_step(ub_ref[0, :, tb * width:(tb + 1) * width], a_ref, bre_ref, bim_ref, xre_ref, xim_ref, 1)

    @pl.when(j == pl.num_programs(1) - 1)
    def _():
        s_ref[0, 0] = xre_ref[0]
        s_ref[0, 1] = xim_ref[0]
        s_ref[0, 2] = xre_ref[1]
        s_ref[0, 3] = xim_ref[1]


def _s5_scan_kernel(uf_ref, ub_ref, x0_ref, a_ref, bre_ref, bim_ref, cre_ref, cim_ref,
                    yf_ref, yb_ref, xre_ref, xim_ref, *, steps, width):
    j = pl.program_id(1)

    @pl.when(j == 0)
    def _():
        xre_ref[0] = x0_ref[0, 0]
        xim_ref[0] = x0_ref[0, 1]
        xre_ref[1] = x0_ref[0, 2]
        xim_ref[1] = x0_ref[0, 3]

    def emit(y_ref, t, d):
        n_out = cre_ref.shape[1]
        rows = cre_ref.shape[2]
        for k in range(n_out):
            x_re = xre_ref[d, :, k * rows:(k + 1) * rows].astype(BF16)
            x_im = xim_ref[d, :, k * rows:(k + 1) * rows].astype(BF16)
            y = (jnp.dot(x_re, cre_ref[d, k], preferred_element_type=F32)
                 - jnp.dot(x_im, cim_ref[d, k], preferred_element_type=F32))
            c0 = t * width + k * SSM_OUT_TILE
            y_ref[0, :, c0:c0 + SSM_OUT_TILE] = y.astype(BF16)

    for t in range(steps):
        _s5_step(uf_ref[0, :, t * width:(t + 1) * width], a_ref, bre_ref, bim_ref, xre_ref, xim_ref, 0)
        emit(yf_ref, t, 0)
        tb = steps - 1 - t
        _s5_step(ub_ref[0, :, tb * width:(tb + 1) * width], a_ref, bre_ref, bim_ref, xre_ref, xim_ref, 1)
        emit(yb_ref, tb, 1)


def _s5_carry_kernel(s_ref, an_ref, x0_ref):
    d = pl.program_id(1)
    n_chunks = s_ref.shape[2]
    a_re = an_ref[0, 0:1, :]
    a_im = an_ref[0, 1:2, :]

    def body(i, carry):
        x_re, x_im = carry
        c = jnp.where(d == 0, i, n_chunks - 1 - i)
        x0_ref[0, 0, pl.ds(c, 1), :] = x_re
        x0_ref[0, 1, pl.ds(c, 1), :] = x_im
        s_re = s_ref[0, 0, pl.ds(c, 1), :]
        s_im = s_ref[0, 1, pl.ds(c, 1), :]
        return a_re * x_re - a_im * x_im + s_re, a_re * x_im + a_im * x_re + s_im

    zero = jnp.zeros((1, s_ref.shape[3]), F32)
    lax.fori_loop(0, n_chunks, body, (zero, zero))


def _s5_branch_core(u, ssm):
    b, l, w = u.shape
    a, a_n, bre, bim, cre, cim = ssm
    n_chunks = min(SSM_CHUNKS, l // SSM_STEPS)
    t_len = l // n_chunks
    steps = min(SSM_STEPS, t_len)
    n_t = t_len // steps
    n_state = a.shape[-1]
    u2 = u.reshape(b, n_chunks, t_len * w)
    blk = steps * w
    fwd = pl.BlockSpec((1, n_chunks, blk), lambda bi, j: (bi, 0, j))
    bwd = pl.BlockSpec((1, n_chunks, blk), lambda bi, j: (bi, 0, n_t - 1 - j))
    state = pl.BlockSpec((1, 4, n_chunks, n_state), lambda bi, j: (bi, 0, 0, 0))
    scratch = [pltpu.VMEM((2, n_chunks, n_state), F32), pltpu.VMEM((2, n_chunks, n_state), F32)]

    s_end = pl.pallas_call(
        functools.partial(_s5_local_kernel, steps=steps, width=w),
        grid=(b, n_t),
        in_specs=[fwd, bwd, _const_spec(a.shape), _const_spec(bre.shape), _const_spec(bim.shape)],
        out_specs=state,
        out_shape=jax.ShapeDtypeStruct((b, 4, n_chunks, n_state), F32),
        scratch_shapes=scratch,
        compiler_params=_params("parallel", "arbitrary"),
        name="s5_local",
    )(u2, u2, a, bre, bim)

    x0 = pl.pallas_call(
        _s5_carry_kernel,
        grid=(b, 2),
        in_specs=[
            pl.BlockSpec((1, 2, n_chunks, n_state), lambda bi, d: (bi, d, 0, 0)),
            pl.BlockSpec((1, 2, n_state), lambda bi, d: (d, 0, 0)),
        ],
        out_specs=pl.BlockSpec((1, 2, n_chunks, n_state), lambda bi, d: (bi, d, 0, 0)),
        out_shape=jax.ShapeDtypeStruct((b, 4, n_chunks, n_state), F32),
        compiler_params=_params("parallel", "arbitrary"),
        name="s5_carry",
    )(s_end, a_n)

    yf, yb = pl.pallas_call(
        functools.partial(_s5_scan_kernel, steps=steps, width=w),
        grid=(b, n_t),
        in_specs=[fwd, bwd, state, _const_spec(a.shape), _const_spec(bre.shape), _const_spec(bim.shape),
                  _const_spec(cre.shape), _const_spec(cim.shape)],
        out_specs=[fwd, bwd],
        out_shape=[jax.ShapeDtypeStruct(u2.shape, BF16), jax.ShapeDtypeStruct(u2.shape, BF16)],
        scratch_shapes=scratch,
        compiler_params=_params("parallel", "arbitrary"),
        name="s5_scan",
    )(u2, u2, x0, a, bre, bim, cre, cim)
    return yf.reshape(b, l, w), yb.reshape(b, l, w)


def _ssm_params(p, n_steps):
    zf = _zoh(p["ssm_lam_re_f"], p["ssm_lam_im_f"], p["ssm_log_step_f"], p["ssm_b_re"], p["ssm_b_im"], n_steps)
    zb = _zoh(p["ssm_lam_re_b"], p["ssm_lam_im_b"], p["ssm_log_step_b"], p["ssm_b_re"], p["ssm_b_im"], n_steps)
    flat = lambda z: z.reshape(1, -1)
    a = jnp.stack([jnp.concatenate([flat(z[0]), flat(z[1])]) for z in (zf, zb)])
    a_n = jnp.stack([jnp.concatenate([flat(z[2]), flat(z[3])]) for z in (zf, zb)])
    bre = jnp.stack([_ssm_in_tiles(zf[4]), _ssm_in_tiles(zb[4])])
    bim = jnp.stack([_ssm_in_tiles(zf[5]), _ssm_in_tiles(zb[5])])
    cre = jnp.stack([_ssm_out_tiles(p["ssm_c_re_f"]), _ssm_out_tiles(p["ssm_c_re_b"])])
    cim = jnp.stack([_ssm_out_tiles(p["ssm_c_im_f"]), _ssm_out_tiles(p["ssm_c_im_b"])])
    return a, a_n, bre, bim, cre, cim


def _merge_kernel(x_ref, mod_ref, u_ref, yf_ref, yb_ref, attn_ref, ga_ref, gs_ref,
                  dskip_ref, wglu_ref, bglu_ref, wpa_ref, wps_ref, wout_ref, gpost_ref, o_ref):
    u = u_ref[0].astype(F32)
    y = yf_ref[0].astype(F32) + yb_ref[0].astype(F32) + u * dskip_ref[...]
    yg = jax.nn.gelu(y)
    gate = jax.nn.sigmoid(jnp.dot(yg.astype(BF16), wglu_ref[...], preferred_element_type=F32) + bglu_ref[...])
    ssm = (yg * gate).astype(BF16)
    pa = jnp.dot(attn_ref[0], wpa_ref[...], preferred_element_type=F32)
    ps = jnp.dot(ssm, wps_ref[...], preferred_element_type=F32)
    merged = ga_ref[0].astype(F32) * pa + gs_ref[0].astype(F32) * ps
    mixed = jnp.dot(merged.astype(BF16), wout_ref[...], preferred_element_type=F32)
    o_ref[0] = x_ref[0] + mod_ref[0, 2:3, :] * (_rms(mixed) * gpost_ref[...])


def _merge(x, mod, u, yf, yb, attn, ga, gs, dskip, wglu, bglu, wpa, wps, wout, gpost):
    b, l, d = x.shape
    w = u.shape[-1]
    tm = min(TOKEN_TILE, l)
    tok = lambda width: pl.BlockSpec((1, tm, width), lambda bi, i: (bi, i, 0))
    return pl.pallas_call(
        _merge_kernel,
        grid=(b, l // tm),
        in_specs=[
            tok(d), pl.BlockSpec((1, N_MOD, d), lambda bi, i: (bi, 0, 0)),
            tok(w), tok(w), tok(w), tok(d), tok(d), tok(d),
            _const_spec((1, w)), _const_spec(wglu.shape), _const_spec((1, w)),
            _const_spec(wpa.shape), _const_spec(wps.shape), _const_spec(wout.shape), _const_spec((1, d)),
        ],
        out_specs=tok(d),
        out_shape=jax.ShapeDtypeStruct((b, l, d), F32),
        compiler_params=_params("parallel", "parallel"),
        name="merge",
    )(x, mod, u, yf, yb, attn, ga, gs, dskip, wglu, bglu, wpa, wps, wout, gpost)


def _ffn_kernel(x_ref, mod_ref, gpre_ref, w1_ref, w2_ref, gpost_ref, o_ref):
    x = x_ref[0]
    h = _rms(x) * gpre_ref[...] * (1.0 + mod_ref[0, 4:5, :]) + mod_ref[0, 3:4, :]
    a = jnp.dot(h.astype(BF16), w1_ref[...], preferred_element_type=F32)
    a = jnp.square(jnp.maximum(a, 0.0)).astype(BF16)
    f = jnp.dot(a, w2_ref[...], preferred_element_type=F32)
    o_ref[0] = x + mod_ref[0, 5:6, :] * (_rms(f) * gpost_ref[...])


def _ffn(x, mod, gpre, w1, w2, gpost):
    b, l, d = x.shape
    tm = min(TOKEN_TILE, l)
    tok = pl.BlockSpec((1, tm, d), lambda bi, i: (bi, i, 0))
    return pl.pallas_call(
        _ffn_kernel,
        grid=(b, l // tm),
        in_specs=[tok, pl.BlockSpec((1, N_MOD, d), lambda bi, i: (bi, 0, 0)), _const_spec((1, d)),
                  _const_spec(w1.shape), _const_spec(w2.shape), _const_spec((1, d))],
        out_specs=tok,
        out_shape=jax.ShapeDtypeStruct((b, l, d), F32),
        compiler_params=_params("parallel", "parallel"),
        name="ffn",
    )(x, mod, gpre, w1, w2, gpost)


def _encoder_layer(x, mod, p, lam_init):
    b, l, d = x.shape
    n_heads = d // V_DIM
    tables = _rope_tables(l)
    q, kt, v, u, ga, gs = _in_proj(x, mod, p["g_pre_mix"], p["wq"], p["wkt"], p["wv"], p["wu"], p["wg"],
                                   tables, n_heads=n_heads)
    attn = _diff_attn(q, kt, v, p["lam_vecs"], p["g_head"], lam_init=lam_init)
    n_chunks = min(SSM_CHUNKS, l // SSM_STEPS)
    yf, yb = _s5_branch_core(u, _ssm_params(p, l // n_chunks))
    x1 = _merge(x, mod, u, yf, yb, attn, ga, gs, p["ssm_d"], p["w_glu"], p["b_glu"],
                p["w_proj_attn"], p["w_proj_ssm"], p["w_out"], p["g_post_mix"])
    return _ffn(x1, mod, p["g_pre_ffn"], p["w_ff1"], p["w_ff2"], p["g_post_ffn"])


def kernel(x_prompt, x_sample, c_prompt, c_sample, w_ada, b_ada, g_pre_mix, g_post_mix, g_pre_ffn, g_post_ffn, w_in, lam_q1, lam_k1, lam_q2, lam_k2, g_head, ssm_lam_re_f, ssm_lam_im_f, ssm_log_step_f, ssm_c_re_f, ssm_c_im_f, ssm_lam_re_b, ssm_lam_im_b, ssm_log_step_b, ssm_c_re_b, ssm_c_im_b, ssm_b_re, ssm_b_im, ssm_d, w_glu, b_glu, w_proj_attn, w_proj_ssm, w_out, w_ff1, w_ff2):
    depth = w_in.shape[0]
    d = x_prompt.shape[-1]
    q_width = 2 * QK_DIM * (d // V_DIM)
    ssm_width = w_glu.shape[-1]
    splits = (q_width, 2 * q_width, 2 * q_width + d, 2 * q_width + d + ssm_width)
    n_p, n_s = c_prompt.shape[0], c_sample.shape[0]
    pad = -(n_p + n_s) % 16
    c_all = jnp.concatenate([c_prompt, c_sample, jnp.zeros((pad, d), F32)], axis=0)

    y_prompt, y_sample = x_prompt, x_sample
    for layer in range(depth):
        row = lambda t: t[layer].reshape(1, -1).astype(F32)
        w_in_l = w_in[layer]
        p = dict(
            g_pre_mix=row(g_pre_mix), g_post_mix=row(g_post_mix), g_pre_ffn=row(g_pre_ffn),
            g_post_ffn=row(g_post_ffn), g_head=row(g_head),
            wq=w_in_l[:, :splits[0]].astype(BF16),
            wkt=w_in_l[:, splits[0]:splits[1]].T.astype(BF16),
            wv=w_in_l[:, splits[1]:splits[2]].astype(BF16),
            wu=w_in_l[:, splits[2]:splits[3]].astype(BF16),
            wg=w_in_l[:, splits[3]:].astype(BF16),
            lam_vecs=(row(lam_q1), row(lam_k1), row(lam_q2), row(lam_k2)),
            ssm_lam_re_f=ssm_lam_re_f[layer], ssm_lam_im_f=ssm_lam_im_f[layer], ssm_log_step_f=ssm_log_step_f[layer],
            ssm_c_re_f=ssm_c_re_f[layer], ssm_c_im_f=ssm_c_im_f[layer],
            ssm_lam_re_b=ssm_lam_re_b[layer], ssm_lam_im_b=ssm_lam_im_b[layer], ssm_log_step_b=ssm_log_step_b[layer],
            ssm_c_re_b=ssm_c_re_b[layer], ssm_c_im_b=ssm_c_im_b[layer],
            ssm_b_re=ssm_b_re[layer], ssm_b_im=ssm_b_im[layer],
            ssm_d=row(ssm_d), w_glu=w_glu[layer].astype(BF16), b_glu=row(b_glu),
            w_proj_attn=w_proj_attn[layer].astype(BF16), w_proj_ssm=w_proj_ssm[layer].astype(BF16),
            w_out=w_out[layer].astype(BF16), w_ff1=w_ff1[layer].astype(BF16), w_ff2=w_ff2[layer].astype(BF16),
        )
        lam_init = 0.8 - 0.6 * math.exp(-0.3 * layer)
        mod = _adaln_mod(c_all, w_ada[layer], b_ada[layer])[:n_p + n_s].reshape(n_p + n_s, N_MOD, d)
        y_prompt = _encoder_layer(y_prompt, mod[:n_p], p, lam_init)
        y_sample = _encoder_layer(y_sample, mod[n_p:], p, lam_init)
    return (y_prompt, y_sample)
```

```python
import functools
import math

import jax
import jax.numpy as jnp
from jax import lax
from jax.experimental import pallas as pl
from jax.experimental.pallas import tpu as pltpu

F32 = jnp.float32
BF16 = jnp.bfloat16

QK_DIM = 64
V_DIM = 128
ROT_DIM = 16
ROT_HALF = ROT_DIM // 2
ROPE_THETA = 500000.0
SSM_GROUP_SIZE = 16
SSM_STATE = 64
N_MOD = 6
EPS = 1e-6
LOG2E = 1.4426950408889634

LANES = 128
MXU_DIM = 256
VMEM_LIMIT_BYTES = 56 * 1024 * 1024

TOKEN_TILE = 512
Q_TILE = 512
KV_TILE = 2048
SSM_CHUNKS = 256
SSM_STEPS = 4
STATE_TILE = 256
SSM_OUT_TILE = 256

SHIFT_HEADROOM = 30.0
SHIFT_MAX_GAP = 90.0
NORM_SLACK = 1.001
KV_UNROLL = 4
NORM_CHUNK = 2048


def _params(*semantics):
    return pltpu.CompilerParams(dimension_semantics=semantics, vmem_limit_bytes=VMEM_LIMIT_BYTES)


def _const_spec(shape):
    nd = len(shape)
    return pl.BlockSpec(shape, lambda *_: (0,) * nd, pipeline_mode=pl.Buffered(1))


def _split_bf16(x):
    hi = x.astype(BF16)
    return hi, (x - hi.astype(F32)).astype(BF16)


def _rms(x):
    return x * lax.rsqrt(jnp.mean(x * x, axis=-1, keepdims=True) + EPS)


def _adaln_kernel(c_ref, w_ref, b_ref, o_ref):
    c = c_ref[...]
    a = c * jax.nn.sigmoid(c)
    w = w_ref[...]
    a_hi = a.astype(BF16)
    a_lo = (a - a_hi.astype(F32)).astype(BF16)
    w_hi = w.astype(BF16)
    w_lo = (w - w_hi.astype(F32)).astype(BF16)
    acc = jnp.dot(a_hi, w_hi, preferred_element_type=F32)
    acc += jnp.dot(a_lo, w_hi, preferred_element_type=F32)
    acc += jnp.dot(a_hi, w_lo, preferred_element_type=F32)
    o_ref[...] = acc + b_ref[...]


def _adaln_mod(c_all, w_ada, b_ada):
    rows, d = c_all.shape
    n = w_ada.shape[1]
    tn = n // 4
    return pl.pallas_call(
        _adaln_kernel,
        grid=(n // tn,),
        in_specs=[
            pl.BlockSpec((rows, d), lambda j: (0, 0)),
            pl.BlockSpec((d, tn), lambda j: (0, j)),
            pl.BlockSpec((1, tn), lambda j: (0, j)),
        ],
        out_specs=pl.BlockSpec((rows, tn), lambda j: (0, j)),
        out_shape=jax.ShapeDtypeStruct((rows, n), F32),
        compiler_params=_params("arbitrary"),
        name="adaln_mod",
    )(c_all, w_ada, b_ada.reshape(1, n))


def _in_proj_kernel(x_ref, mod_ref, g_ref, wq_ref, wkt_ref, wv_ref, wu_ref, wg_ref,
                    ct_ref, sa_ref, sb_ref, cost_ref, sint_ref,
                    q_ref, kt_ref, v_ref, u_ref, ga_ref, gs_ref, *, n_heads, q_scale):
    x = x_ref[0]
    h = _rms(x) * g_ref[...] * (1.0 + mod_ref[0, 1:2, :]) + mod_ref[0, 0:1, :]
    hb = h.astype(BF16)

    q = jnp.dot(hb, wq_ref[...], preferred_element_type=F32)
    ct, sa, sb = ct_ref[...], sa_ref[...], sb_ref[...]
    for hd in range(n_heads):
        t = q[:, hd * LANES:(hd + 1) * LANES]
        r = t * ct + pltpu.roll(t, LANES - ROT_HALF, 1) * sa + pltpu.roll(t, ROT_HALF, 1) * sb
        q_ref[0, hd] = (r * q_scale).astype(BF16)

    kt = lax.dot_general(wkt_ref[...], hb, (((1,), (1,)), ((), ())), preferred_element_type=F32)
    cos_t, sin_t = cost_ref[...], sint_ref[...]
    for hd in range(n_heads):
        for m in range(2):
            r0 = hd * LANES + m * QK_DIM
            x1 = kt[r0:r0 + ROT_HALF]
            x2 = kt[r0 + ROT_HALF:r0 + ROT_DIM]
            rot = jnp.concatenate([x1 * cos_t - x2 * sin_t, x2 * cos_t + x1 * sin_t], axis=0)
            kt_ref[0, hd, m * QK_DIM:m * QK_DIM + ROT_DIM, :] = rot.astype(BF16)
            kt_ref[0, hd, m * QK_DIM + ROT_DIM:(m + 1) * QK_DIM, :] = kt[r0 + ROT_DIM:r0 + QK_DIM].astype(BF16)

    v = jnp.dot(hb, wv_ref[...], preferred_element_type=F32)
    for hd in range(n_heads):
        v_ref[0, hd] = v[:, hd * LANES:(hd + 1) * LANES].astype(BF16)

    u_ref[0] = jnp.dot(hb, wu_ref[...], preferred_element_type=F32).astype(BF16)

    d = ga_ref.shape[-1]
    gates = jax.nn.sigmoid(jnp.dot(hb, wg_ref[...], preferred_element_type=F32))
    ga_ref[0] = gates[:, :d].astype(BF16)
    gs_ref[0] = gates[:, d:].astype(BF16)


def _in_proj(x, mod, g_pre, wq, wkt, wv, wu, wg, tables, *, n_heads):
    b, l, d = x.shape
    tm = min(TOKEN_TILE, l)
    ct, sa, sb, cos_t, sin_t = tables
    ssm_w = wu.shape[1]
    kern = functools.partial(_in_proj_kernel, n_heads=n_heads, q_scale=(QK_DIM ** -0.5) * LOG2E)
    tok = lambda w: pl.BlockSpec((1, tm, w), lambda bi, i: (bi, i, 0))
    return pl.pallas_call(
        kern,
        grid=(b, l // tm),
        in_specs=[
            tok(d),
            pl.BlockSpec((1, N_MOD, d), lambda bi, i: (bi, 0, 0)),
            _const_spec((1, d)),
            _const_spec(wq.shape), _const_spec(wkt.shape), _const_spec(wv.shape),
            _const_spec(wu.shape), _const_spec(wg.shape),
            pl.BlockSpec((tm, LANES), lambda bi, i: (i, 0)),
            pl.BlockSpec((tm, LANES), lambda bi, i: (i, 0)),
            pl.BlockSpec((tm, LANES), lambda bi, i: (i, 0)),
            pl.BlockSpec((ROT_HALF, tm), lambda bi, i: (0, i)),
            pl.BlockSpec((ROT_HALF, tm), lambda bi, i: (0, i)),
        ],
        out_specs=[
            pl.BlockSpec((1, n_heads, tm, LANES), lambda bi, i: (bi, 0, i, 0)),
            pl.BlockSpec((1, n_heads, LANES, tm), lambda bi, i: (bi, 0, 0, i)),
            pl.BlockSpec((1, n_heads, tm, LANES), lambda bi, i: (bi, 0, i, 0)),
            tok(ssm_w), tok(d), tok(d),
        ],
        out_shape=[
            jax.ShapeDtypeStruct((b, n_heads, l, LANES), BF16),
            jax.ShapeDtypeStruct((b, n_heads, LANES, l), BF16),
            jax.ShapeDtypeStruct((b, n_heads, l, LANES), BF16),
            jax.ShapeDtypeStruct((b, l, ssm_w), BF16),
            jax.ShapeDtypeStruct((b, l, d), BF16),
            jax.ShapeDtypeStruct((b, l, d), BF16),
        ],
        compiler_params=_params("parallel", "parallel"),
        name="in_proj",
    )(x, mod, g_pre, wq, wkt, wv, wu, wg, ct, sa, sb, cos_t, sin_t)


def _rope_tables(l):
    pos = jnp.arange(l, dtype=F32)
    inv_freq = ROPE_THETA ** (-jnp.arange(0, ROT_DIM, 2, dtype=F32) / ROT_DIM)
    ang = pos[:, None] * inv_freq[None, :]
    cos, sin = jnp.cos(ang), jnp.sin(ang)
    lane = jnp.arange(LANES) % QK_DIM
    cos_l = jnp.take(cos, lane % ROT_HALF, axis=1)
    sin_l = jnp.take(sin, lane % ROT_HALF, axis=1)
    ct = jnp.where(lane < ROT_DIM, cos_l, 1.0)
    sa = jnp.where(lane < ROT_HALF, -sin_l, 0.0)
    sb = jnp.where((lane >= ROT_HALF) & (lane < ROT_DIM), sin_l, 0.0)
    return ct, sa, sb, cos.T, sin.T


def _attn_kernel(lq1_ref, lk1_ref, lq2_ref, lk2_ref, gh_ref, q_ref, kt_ref, v_ref, o_ref,
                 kn_ref, krep_ref, m_ref, acc_ref, *, tk, lam_init):
    tq = q_ref.shape[2]
    l_keys = kt_ref.shape[-1]
    nk = l_keys // tk

    @pl.when(pl.program_id(2) == 0)
    def _key_norms():
        chunk = min(NORM_CHUNK, l_keys)

        def nbody(c, carry):
            off = pl.multiple_of(c * chunk, chunk)
            kt = kt_ref[0, 0, :, pl.ds(off, chunk)].astype(F32)
            sq = kt * kt
            return (jnp.maximum(carry[0], jnp.sum(sq[:QK_DIM], axis=0, keepdims=True)),
                    jnp.maximum(carry[1], jnp.sum(sq[QK_DIM:], axis=0, keepdims=True)))

        zero = jnp.zeros((1, chunk), F32)
        n0, n1 = lax.fori_loop(0, l_keys // chunk, nbody, (zero, zero))
        kn_ref[0:1, :] = jnp.broadcast_to(jnp.sqrt(jnp.max(n0, axis=1, keepdims=True)), (1, LANES))
        kn_ref[1:2, :] = jnp.broadcast_to(jnp.sqrt(jnp.max(n1, axis=1, keepdims=True)), (1, LANES))
        pick_first = (lax.broadcasted_iota(jnp.int32, (LANES, LANES), 0) == 0).astype(BF16)
        krep_ref[...] = jnp.dot(kt_ref[0, 0, :, 0:LANES], pick_first, preferred_element_type=F32).astype(BF16)

    q = q_ref[0, 0]
    first_map = lax.broadcasted_iota(jnp.int32, (tq, LANES), 1) < QK_DIM
    zero = jnp.zeros_like(q)
    q_maps = (jnp.where(first_map, q, zero), jnp.where(first_map, zero, q))
    qf = q.astype(F32)
    sq_hi, sq_lo = _split_bf16(qf * qf)
    dim_in_first = lax.broadcasted_iota(jnp.int32, (LANES, LANES), 0) < QK_DIM
    gap = None
    for mi in range(2):
        sel = (dim_in_first if mi == 0 else jnp.logical_not(dim_in_first)).astype(BF16)
        q_norm = jnp.sqrt(jnp.dot(sq_hi, sel, preferred_element_type=F32)
                          + jnp.dot(sq_lo, sel, preferred_element_type=F32)) * NORM_SLACK
        upper = q_norm * kn_ref[mi:mi + 1, :]
        attained = jnp.dot(q_maps[mi], krep_ref[...], preferred_element_type=F32)
        m_ref[mi] = jnp.maximum(attained, upper - SHIFT_HEADROOM)
        g = upper - attained
        gap = g if gap is None else jnp.maximum(gap, g)

    @pl.when(jnp.max(gap) > SHIFT_MAX_GAP)
    def _exact_shift():
        def mbody(j, carry):
            off = pl.multiple_of(j * tk, tk)
            kt = kt_ref[0, 0, :, pl.ds(off, tk)]
            out = []
            for mi in range(2):
                s = jnp.dot(q_maps[mi], kt, preferred_element_type=F32)
                mx = carry[mi]
                for c in range(tk // LANES):
                    mx = jnp.maximum(mx, s[:, c * LANES:(c + 1) * LANES])
                out.append(mx)
            return tuple(out)

        neg = jnp.full((tq, LANES), -jnp.inf, F32)
        mx = lax.fori_loop(0, nk, mbody, (neg, neg))
        for mi in range(2):
            m_ref[mi] = jnp.broadcast_to(jnp.max(mx[mi], axis=1, keepdims=True), (tq, LANES))

    acc_ref[...] = jnp.zeros_like(acc_ref)
    ones = jnp.ones((tk, LANES), BF16)

    def body(j, carry):
        off = pl.multiple_of(j * tk, tk)
        kt = kt_ref[0, 0, :, pl.ds(off, tk)]
        va = jnp.concatenate([v_ref[0, 0, pl.ds(off, tk), :], ones], axis=1)
        for mi in range(2):
            s = jnp.dot(q_maps[mi], kt, preferred_element_type=F32)
            m = m_ref[mi]
            p = jnp.concatenate([jnp.exp2(s[:, c * LANES:(c + 1) * LANES] - m) for c in range(tk // LANES)],
                                axis=1)
            acc_ref[mi] += jnp.dot(p.astype(BF16), va, preferred_element_type=F32)
        return carry

    lax.fori_loop(0, nk, body, 0, unroll=min(KV_UNROLL, nk))

    lam = (jnp.exp(jnp.sum(lq1_ref[...] * lk1_ref[...], axis=-1, keepdims=True))
           - jnp.exp(jnp.sum(lq2_ref[...] * lk2_ref[...], axis=-1, keepdims=True)) + lam_init)
    acc0 = acc_ref[0]
    acc1 = acc_ref[1]
    o = acc0[:, :LANES] / acc0[:, LANES:] - lam * (acc1[:, :LANES] / acc1[:, LANES:])
    o2_hi, o2_lo = _split_bf16(o * o)
    ones_sq = jnp.ones((LANES, LANES), BF16)
    mean_sq = (jnp.dot(o2_hi, ones_sq, preferred_element_type=F32)
               + jnp.dot(o2_lo, ones_sq, preferred_element_type=F32)) * (1.0 / V_DIM)
    o_ref[0] = (o * lax.rsqrt(mean_sq + EPS) * gh_ref[...] * (1.0 - lam_init)).astype(BF16)


def _diff_attn(q, kt, v, lam_vecs, g_head, *, lam_init):
    b, n_heads, l, _ = q.shape
    tq = min(Q_TILE, l)
    tk = min(KV_TILE, l)
    kern = functools.partial(_attn_kernel, tk=tk, lam_init=lam_init)
    vec = _const_spec((1, QK_DIM))
    return pl.pallas_call(
        kern,
        grid=(b, n_heads, l // tq),
        in_specs=[
            vec, vec, vec, vec, _const_spec((1, V_DIM)),
            pl.BlockSpec((1, 1, tq, LANES), lambda bi, h, i: (bi, h, i, 0)),
            pl.BlockSpec((1, 1, LANES, l), lambda bi, h, i: (bi, h, 0, 0)),
            pl.BlockSpec((1, 1, l, LANES), lambda bi, h, i: (bi, h, 0, 0)),
        ],
        out_specs=pl.BlockSpec((1, tq, LANES), lambda bi, h, i: (bi, i, h)),
        out_shape=jax.ShapeDtypeStruct((b, l, n_heads * V_DIM), BF16),
        scratch_shapes=[
            pltpu.VMEM((2, LANES), F32),
            pltpu.VMEM((LANES, LANES), BF16),
            pltpu.VMEM((2, tq, LANES), F32),
            pltpu.VMEM((2, tq, 2 * LANES), F32),
        ],
        compiler_params=_params("parallel", "parallel", "arbitrary"),
        name="diff_attn",
    )(*lam_vecs, g_head, q, kt, v)


def _zoh(lam_re, lam_im, log_step, b_re, b_im, n_steps):
    step = jnp.exp(log_step.astype(F32))[:, None]
    lam_re = lam_re.astype(F32)
    lam_im = lam_im.astype(F32)
    er = jnp.exp(lam_re * step)
    ar = er * jnp.cos(lam_im * step)
    ai = er * jnp.sin(lam_im * step)
    en = jnp.exp(lam_re * step * n_steps)
    anr = en * jnp.cos(lam_im * step * n_steps)
    ani = en * jnp.sin(lam_im * step * n_steps)
    den = lam_re * lam_re + lam_im * lam_im
    nr = ar - 1.0
    cr = ((nr * lam_re + ai * lam_im) / den)[..., None]
    ci = ((ai * lam_re - nr * lam_im) / den)[..., None]
    b_re = b_re.astype(F32)
    b_im = b_im.astype(F32)
    return ar, ai, anr, ani, cr * b_re - ci * b_im, cr * b_im + ci * b_re


def _ssm_in_tiles(bb):
    g, n, p = bb.shape
    eye = jnp.eye(g, dtype=F32)
    full = jnp.einsum("gnp,gh->gphn", bb, eye).reshape(g * p, g * n)
    tiles = []
    for k in range(g * n // STATE_TILE):
        r0 = (k * STATE_TILE // n) * p // LANES * LANES
        tiles.append(full[r0:r0 + LANES, k * STATE_TILE:(k + 1) * STATE_TILE])
    return jnp.stack(tiles).astype(BF16)


def _ssm_out_tiles(c):
    g, p, n = c.shape
    eye = jnp.eye(g, dtype=F32)
    full = jnp.einsum("gpn,gh->gnhp", c.astype(F32), eye).reshape(g * n, g * p)
    rows = SSM_OUT_TILE // p * n
    tiles = [full[k * rows:(k + 1) * rows, k * SSM_OUT_TILE:(k + 1) * SSM_OUT_TILE]
             for k in range(g * p // SSM_OUT_TILE)]
    return jnp.stack(tiles).astype(BF16)


def _s5_step(u_t, a_ref, bre_ref, bim_ref, xre_ref, xim_ref, d):
    n_tiles = bre_ref.shape[1]
    in_rows_per_tile = STATE_TILE // SSM_STATE * SSM_GROUP_SIZE
    for k in range(n_tiles):
        c0 = (k * in_rows_per_tile) // LANES * LANES
        uk = u_t[:, c0:c0 + LANES]
        sl = slice(k * STATE_TILE, (k + 1) * STATE_TILE)
        bu_re = jnp.dot(uk, bre_ref[d, k], preferred_element_type=F32)
        bu_im = jnp.dot(uk, bim_ref[d, k], preferred_element_type=F32)
        a_re = a_ref[d, 0:1, sl]
        a_im = a_ref[d, 1:2, sl]
        x_re = xre_ref[d, :, sl]
        x_im = xim_ref[d, :, sl]
        xre_ref[d, :, sl] = a_re * x_re - a_im * x_im + bu_re
        xim_ref[d, :, sl] = a_re * x_im + a_im * x_re + bu_im


def _s5_local_kernel(uf_ref, ub_ref, a_ref, bre_ref, bim_ref, s_ref, xre_ref, xim_ref, *, steps, width):
    j = pl.program_id(1)

    @pl.when(j == 0)
    def _():
        xre_ref[...] = jnp.zeros_like(xre_ref)
        xim_ref[...] = jnp.zeros_like(xim_ref)

    for t in range(steps):
        _s5_step(uf_ref[0, :, t * width:(t + 1) * width], a_ref, bre_ref, bim_ref, xre_ref, xim_ref, 0)
        tb = steps - 1 - t
        _s5_step(ub_ref[0, :, tb * width:(tb + 1) * width], a_ref, bre_ref, bim_ref, xre_ref, xim_ref, 1)

    @pl.when(j == pl.num_programs(1) - 1)
    def _():
        s_ref[0, 0] = xre_ref[0]
        s_ref[0, 1] = xim_ref[0]
        s_ref[0, 2] = xre_ref[1]
        s_ref[0, 3] = xim_ref[1]


def _s5_scan_kernel(uf_ref, ub_ref, x0_ref, a_ref, bre_ref, bim_ref, cre_ref, cim_ref,
                    yf_ref, yb_ref, xre_ref, xim_ref, *, steps, width):
    j = pl.program_id(1)

    @pl.when(j == 0)
    def _():
        xre_ref[0] = x0_ref[0, 0]
        xim_ref[0] = x0_ref[0, 1]
        xre_ref[1] = x0_ref[0, 2]
        xim_ref[1] = x0_ref[0, 3]

    def emit(y_ref, t, d):
        n_out = cre_ref.shape[1]
        rows = cre_ref.shape[2]
        for k in range(n_out):
            x_re = xre_ref[d, :, k * rows:(k + 1) * rows].astype(BF16)
            x_im = xim_ref[d, :, k * rows:(k + 1) * rows].astype(BF16)
            y = (jnp.dot(x_re, cre_ref[d, k], preferred_element_type=F32)
                 - jnp.dot(x_im, cim_ref[d, k], preferred_element_type=F32))
            c0 = t * width + k * SSM_OUT_TILE
            y_ref[0, :, c0:c0 + SSM_OUT_TILE] = y.astype(BF16)

    for t in range(steps):
        _s5_step(uf_ref[0, :, t * width:(t + 1) * width], a_ref, bre_ref, bim_ref, xre_ref, xim_ref, 0)
        emit(yf_ref, t, 0)
        tb = steps - 1 - t
        _s5_step(ub_ref[0, :, tb * width:(tb + 1) * width], a_ref, bre_ref, bim_ref, xre_ref, xim_ref, 1)
        emit(yb_ref, tb, 1)


def _s5_carry_kernel(s_ref, an_ref, x0_ref):
    d = pl.program_id(1)
    n_chunks = s_ref.shape[2]
    a_re = an_ref[0, 0:1, :]
    a_im = an_ref[0, 1:2, :]

    def body(i, carry):
        x_re, x_im = carry
        c = jnp.where(d == 0, i, n_chunks - 1 - i)
        x0_ref[0, 0, pl.ds(c, 1), :] = x_re
        x0_ref[0, 1, pl.ds(c, 1), :] = x_im
        s_re = s_ref[0, 0, pl.ds(c, 1), :]
        s_im = s_ref[0, 1, pl.ds(c, 1), :]
        return a_re * x_re - a_im * x_im + s_re, a_re * x_im + a_im * x_re + s_im

    zero = jnp.zeros((1, s_ref.shape[3]), F32)
    lax.fori_loop(0, n_chunks, body, (zero, zero))


def _s5_branch_core(u, ssm):
    b, l, w = u.shape
    a, a_n, bre, bim, cre, cim = ssm
    n_chunks = min(SSM_CHUNKS, l // SSM_STEPS)
    t_len = l // n_chunks
    steps = min(SSM_STEPS, t_len)
    n_t = t_len // steps
    n_state = a.shape[-1]
    u2 = u.reshape(b, n_chunks, t_len * w)
    blk = steps * w
    fwd = pl.BlockSpec((1, n_chunks, blk), lambda bi, j: (bi, 0, j))
    bwd = pl.BlockSpec((1, n_chunks, blk), lambda bi, j: (bi, 0, n_t - 1 - j))
    state = pl.BlockSpec((1, 4, n_chunks, n_state), lambda bi, j: (bi, 0, 0, 0))
    scratch = [pltpu.VMEM((2, n_chunks, n_state), F32), pltpu.VMEM((2, n_chunks, n_state), F32)]

    s_end = pl.pallas_call(
        functools.partial(_s5_local_kernel, steps=steps, width=w),
        grid=(b, n_t),
        in_specs=[fwd, bwd, _const_spec(a.shape), _const_spec(bre.shape), _const_spec(bim.shape)],
        out_specs=state,
        out_shape=jax.ShapeDtypeStruct((b, 4, n_chunks, n_state), F32),
        scratch_shapes=scratch,
        compiler_params=_params("parallel", "arbitrary"),
        name="s5_local",
    )(u2, u2, a, bre, bim)

    x0 = pl.pallas_call(
        _s5_carry_kernel,
        grid=(b, 2),
        in_specs=[
            pl.BlockSpec((1, 2, n_chunks, n_state), lambda bi, d: (bi, d, 0, 0)),
            pl.BlockSpec((1, 2, n_state), lambda bi, d: (d, 0, 0)),
        ],
        out_specs=pl.BlockSpec((1, 2, n_chunks, n_state), lambda bi, d: (bi, d, 0, 0)),
        out_shape=jax.ShapeDtypeStruct((b, 4, n_chunks, n_state), F32),
        compiler_params=_params("parallel", "arbitrary"),
        name="s5_carry",
    )(s_end, a_n)

    yf, yb = pl.pallas_call(
        functools.partial(_s5_scan_kernel, steps=steps, width=w),
        grid=(b, n_t),
        in_specs=[fwd, bwd, state, _const_spec(a.shape), _const_spec(bre.shape), _const_spec(bim.shape),
                  _const_spec(cre.shape), _const_spec(cim.shape)],
        out_specs=[fwd, bwd],
        out_shape=[jax.ShapeDtypeStruct(u2.shape, BF16), jax.ShapeDtypeStruct(u2.shape, BF16)],
        scratch_shapes=scratch,
        compiler_params=_params("parallel", "arbitrary"),
        name="s5_scan",
    )(u2, u2, x0, a, bre, bim, cre, cim)
    return yf.reshape(b, l, w), yb.reshape(b, l, w)


def _ssm_params(p, n_steps):
    zf = _zoh(p["ssm_lam_re_f"], p["ssm_lam_im_f"], p["ssm_log_step_f"], p["ssm_b_re"], p["ssm_b_im"], n_steps)
    zb = _zoh(p["ssm_lam_re_b"], p["ssm_lam_im_b"], p["ssm_log_step_b"], p["ssm_b_re"], p["ssm_b_im"], n_steps)
    flat = lambda z: z.reshape(1, -1)
    a = jnp.stack([jnp.concatenate([flat(z[0]), flat(z[1])]) for z in (zf, zb)])
    a_n = jnp.stack([jnp.concatenate([flat(z[2]), flat(z[3])]) for z in (zf, zb)])
    bre = jnp.stack([_ssm_in_tiles(zf[4]), _ssm_in_tiles(zb[4])])
    bim = jnp.stack([_ssm_in_tiles(zf[5]), _ssm_in_tiles(zb[5])])
    cre = jnp.stack([_ssm_out_tiles(p["ssm_c_re_f"]), _ssm_out_tiles(p["ssm_c_re_b"])])
    cim = jnp.stack([_ssm_out_tiles(p["ssm_c_im_f"]), _ssm_out_tiles(p["ssm_c_im_b"])])
    return a, a_n, bre, bim, cre, cim


def _merge_kernel(x_ref, mod_ref, u_ref, yf_ref, yb_ref, attn_ref, ga_ref, gs_ref,
                  dskip_ref, wglu_ref, bglu_ref, wpa_ref, wps_ref, wout_ref, gpost_ref, o_ref):
    u = u_ref[0].astype(F32)
    y = yf_ref[0].astype(F32) + yb_ref[0].astype(F32) + u * dskip_ref[...]
    yg = jax.nn.gelu(y)
    gate = jax.nn.sigmoid(jnp.dot(yg.astype(BF16), wglu_ref[...], preferred_element_type=F32) + bglu_ref[...])
    ssm = (yg * gate).astype(BF16)
    pa = jnp.dot(attn_ref[0], wpa_ref[...], preferred_element_type=F32)
    ps = jnp.dot(ssm, wps_ref[...], preferred_element_type=F32)
    merged = ga_ref[0].astype(F32) * pa + gs_ref[0].astype(F32) * ps
    mixed = jnp.dot(merged.astype(BF16), wout_ref[...], preferred_element_type=F32)
    o_ref[0] = x_ref[0] + mod_ref[0, 2:3, :] * (_rms(mixed) * gpost_ref[...])


def _merge(x, mod, u, yf, yb, attn, ga, gs, dskip, wglu, bglu, wpa, wps, wout, gpost):
    b, l, d = x.shape
    w = u.shape[-1]
    tm = min(TOKEN_TILE, l)
    tok = lambda width: pl.BlockSpec((1, tm, width), lambda bi, i: (bi, i, 0))
    return pl.pallas_call(
        _merge_kernel,
        grid=(b, l // tm),
        in_specs=[
            tok(d), pl.BlockSpec((1, N_MOD, d), lambda bi, i: (bi, 0, 0)),
            tok(w), tok(w), tok(w), tok(d), tok(d), tok(d),
            _const_spec((1, w)), _const_spec(wglu.shape), _const_spec((1, w)),
            _const_spec(wpa.shape), _const_spec(wps.shape), _const_spec(wout.shape), _const_spec((1, d)),
        ],
        out_specs=tok(d),
        out_shape=jax.ShapeDtypeStruct((b, l, d), F32),
        compiler_params=_params("parallel", "parallel"),
        name="merge",
    )(x, mod, u, yf, yb, attn, ga, gs, dskip, wglu, bglu, wpa, wps, wout, gpost)


def _ffn_kernel(x_ref, mod_ref, gpre_ref, w1_ref, w2_ref, gpost_ref, o_ref):
    x = x_ref[0]
    h = _rms(x) * gpre_ref[...] * (1.0 + mod_ref[0, 4:5, :]) + mod_ref[0, 3:4, :]
    a = jnp.dot(h.astype(BF16), w1_ref[...], preferred_element_type=F32)
    a = jnp.square(jnp.maximum(a, 0.0)).astype(BF16)
    f = jnp.dot(a, w2_ref[...], preferred_element_type=F32)
    o_ref[0] = x + mod_ref[0, 5:6, :] * (_rms(f) * gpost_ref[...])


def _ffn(x, mod, gpre, w1, w2, gpost):
    b, l, d = x.shape
    tm = min(TOKEN_TILE, l)
    tok = pl.BlockSpec((1, tm, d), lambda bi, i: (bi, i, 0))
    return pl.pallas_call(
        _ffn_kernel,
        grid=(b, l // tm),
        in_specs=[tok, pl.BlockSpec((1, N_MOD, d), lambda bi, i: (bi, 0, 0)), _const_spec((1, d)),
                  _const_spec(w1.shape), _const_spec(w2.shape), _const_spec((1, d))],
        out_specs=tok,
        out_shape=jax.ShapeDtypeStruct((b, l, d), F32),
        compiler_params=_params("parallel", "parallel"),
        name="ffn",
    )(x, mod, gpre, w1, w2, gpost)


def _encoder_layer(x, mod, p, lam_init):
    b, l, d = x.shape
    n_heads = d // V_DIM
    tables = _rope_tables(l)
    q, kt, v, u, ga, gs = _in_proj(x, mod, p["g_pre_mix"], p["wq"], p["wkt"], p["wv"], p["wu"], p["wg"],
                                   tables, n_heads=n_heads)
    attn = _diff_attn(q, kt, v, p["lam_vecs"], p["g_head"], lam_init=lam_init)
    n_chunks = min(SSM_CHUNKS, l // SSM_STEPS)
    yf, yb = _s5_branch_core(u, _ssm_params(p, l // n_chunks))
    x1 = _merge(x, mod, u, yf, yb, attn, ga, gs, p["ssm_d"], p["w_glu"], p["b_glu"],
                p["w_proj_attn"], p["w_proj_ssm"], p["w_out"], p["g_post_mix"])
    return _ffn(x1, mod, p["g_pre_ffn"], p["w_ff1"], p["w_ff2"], p["g_post_ffn"])


def kernel(x_prompt, x_sample, c_prompt, c_sample, w_ada, b_ada, g_pre_mix, g_post_mix, g_pre_ffn, g_post_ffn, w_in, lam_q1, lam_k1, lam_q2, lam_k2, g_head, ssm_lam_re_f, ssm_lam_im_f, ssm_log_step_f, ssm_c_re_f, ssm_c_im_f, ssm_lam_re_b, ssm_lam_im_b, ssm_log_step_b, ssm_c_re_b, ssm_c_im_b, ssm_b_re, ssm_b_im, ssm_d, w_glu, b_glu, w_proj_attn, w_proj_ssm, w_out, w_ff1, w_ff2):
    depth = w_in.shape[0]
    d = x_prompt.shape[-1]
    q_width = 2 * QK_DIM * (d // V_DIM)
    ssm_width = w_glu.shape[-1]
    splits = (q_width, 2 * q_width, 2 * q_width + d, 2 * q_width + d + ssm_width)
    n_p, n_s = c_prompt.shape[0], c_sample.shape[0]
    pad = -(n_p + n_s) % 16
    c_all = jnp.concatenate([c_prompt, c_sample, jnp.zeros((pad, d), F32)], axis=0)

    y_prompt, y_sample = x_prompt, x_sample
    for layer in range(depth):
        row = lambda t: t[layer].reshape(1, -1).astype(F32)
        w_in_l = w_in[layer]
        p = dict(
            g_pre_mix=row(g_pre_mix), g_post_mix=row(g_post_mix), g_pre_ffn=row(g_pre_ffn),
            g_post_ffn=row(g_post_ffn), g_head=row(g_head),
            wq=w_in_l[:, :splits[0]].astype(BF16),
            wkt=w_in_l[:, splits[0]:splits[1]].T.astype(BF16),
            wv=w_in_l[:, splits[1]:splits[2]].astype(BF16),
            wu=w_in_l[:, splits[2]:splits[3]].astype(BF16),
            wg=w_in_l[:, splits[3]:].astype(BF16),
            lam_vecs=(row(lam_q1), row(lam_k1), row(lam_q2), row(lam_k2)),
            ssm_lam_re_f=ssm_lam_re_f[layer], ssm_lam_im_f=ssm_lam_im_f[layer], ssm_log_step_f=ssm_log_step_f[layer],
            ssm_c_re_f=ssm_c_re_f[layer], ssm_c_im_f=ssm_c_im_f[layer],
            ssm_lam_re_b=ssm_lam_re_b[layer], ssm_lam_im_b=ssm_lam_im_b[layer], ssm_log_step_b=ssm_log_step_b[layer],
            ssm_c_re_b=ssm_c_re_b[layer], ssm_c_im_b=ssm_c_im_b[layer],
            ssm_b_re=ssm_b_re[layer], ssm_b_im=ssm_b_im[layer],
            ssm_d=row(ssm_d), w_glu=w_glu[layer].astype(BF16), b_glu=row(b_glu),
            w_proj_attn=w_proj_attn[layer].astype(BF16), w_proj_ssm=w_proj_ssm[layer].astype(BF16),
            w_out=w_out[layer].astype(BF16), w_ff1=w_ff1[layer].astype(BF16), w_ff2=w_ff2[layer].astype(BF16),
        )
        lam_init = 0.8 - 0.6 * math.exp(-0.3 * layer)
        mod = _adaln_mod(c_all, w_ada[layer], b_ada[layer])[:n_p + n_s].reshape(n_p + n_s, N_MOD, d)
        y_prompt = _encoder_layer(y_prompt, mod[:n_p], p, lam_init)
        y_sample = _encoder_layer(y_sample, mod[n_p:], p, lam_init)
    return (y_prompt, y_sample)
```
